```python
import jax, jax.numpy as jnp
from jax import lax
import numpy as np

D_MODEL = 2048
BATCH = 16
SEQ = 2048
DEPTH = 4

GRID_W = 64
CTX_LEN = 256
N_EVEN = (DEPTH + 1) // 2
N_ODD = DEPTH // 2

POOL_WINDOWS = (2, 4, 8, 16)
N_POOL = 4
POOL_W = D_MODEL // 2
POOL_GROUP = POOL_W // N_POOL
RWKV_W = D_MODEL // 2
RWKV_HEAD = 64
RWKV_HEADS = RWKV_W // RWKV_HEAD
LORA_W = 64
GN_EPS = 64e-5
ATT_HEADS = 16
HEAD_DIM = D_MODEL // ATT_HEADS
KV_HEADS = 4
GROUP = ATT_HEADS // KV_HEADS
ATT_W = ATT_HEADS * HEAD_DIM
KV_W = KV_HEADS * HEAD_DIM
AXIS_DIM = HEAD_DIM // 2
ROPE_THETA = 10000.0
Q_BLOCK = 128
EV_GATE_W = POOL_W + RWKV_W
EV_IN = POOL_W + 3 * RWKV_W + 4 * LORA_W + EV_GATE_W
EV_SPLITS = (POOL_W, POOL_W + RWKV_W, POOL_W + 2 * RWKV_W, POOL_W + 3 * RWKV_W,
             POOL_W + 3 * RWKV_W + 2 * LORA_W, POOL_W + 3 * RWKV_W + 4 * LORA_W)
OD_IN = ATT_W + 2 * KV_W + ATT_W
DEEPNORM_ALPHA = (2 * DEPTH) ** 0.25
DEEPNORM_BETA = (8 * DEPTH) ** -0.25

kernel_name = 'hybrid_pool_rwkv7_gqa_diffusion_trunk'


def layer_norm(x, g, b, eps=1e-6):
    xf = x.astype(jnp.float32)
    mu = jnp.mean(xf, -1, keepdims=True)
    var = jnp.mean(jnp.square(xf - mu), -1, keepdims=True)
    return ((xf - mu) * lax.rsqrt(var + eps)).astype(x.dtype) * g + b


def rms_norm(x, g, eps=1e-6):
    xf = x.astype(jnp.float32)
    return (xf * lax.rsqrt(jnp.mean(jnp.square(xf), -1, keepdims=True) + eps)).astype(x.dtype) * g


def axial_rope_tables(rows):
    row = jnp.repeat(jnp.arange(rows, dtype=jnp.float32), GRID_W)
    col = jnp.tile(jnp.arange(GRID_W, dtype=jnp.float32), rows)
    inv_freq = ROPE_THETA ** (-jnp.arange(0, AXIS_DIM, 2, dtype=jnp.float32) / AXIS_DIM)
    ang = jnp.stack([row[:, None] * inv_freq, col[:, None] * inv_freq], axis=1)
    return jnp.cos(ang), jnp.sin(ang)


def apply_axial_rope(x, cos, sin):
    B, T, H, _ = x.shape
    xr = x.astype(jnp.float32).reshape(B, T, H, 2, 2, AXIS_DIM // 2)
    x1, x2 = xr[..., 0, :], xr[..., 1, :]
    c = cos[None, :, None]
    s = sin[None, :, None]
    out = jnp.stack([x1 * c - x2 * s, x2 * c + x1 * s], axis=-2)
    return out.reshape(x.shape).astype(x.dtype)


def attention(q, k, v):
    s = jnp.einsum('bqhgd,bkhd->bhgqk', q, k, preferred_element_type=jnp.float32) * (HEAD_DIM ** -0.5)
    p = jax.nn.softmax(s, axis=-1).astype(v.dtype)
    return jnp.einsum('bhgqk,bkhd->bqhgd', p, v)


def blocked_attention(q, k, v):
    B, T = q.shape[:2]
    qb = q.reshape(B, T // Q_BLOCK, Q_BLOCK, KV_HEADS, GROUP, HEAD_DIM).swapaxes(0, 1)
    ob = lax.map(lambda qi: attention(qi, k, v), qb)
    return ob.swapaxes(0, 1).reshape(B, T, KV_HEADS, GROUP, HEAD_DIM)


def multiscale_pool(u, w_grp, scale):
    B, T, _ = u.shape
    ug = u.astype(jnp.float32).reshape(B, T, N_POOL, POOL_GROUP)
    cs = jnp.concatenate([jnp.zeros((B, 1, N_POOL, POOL_GROUP), jnp.float32), jnp.cumsum(ug, axis=1)], axis=1)
    t = jnp.arange(T)
    means = []
    for g, w in enumerate(POOL_WINDOWS):
        lo = jnp.clip(t - w // 2, 0, T)
        hi = jnp.clip(t - w // 2 + w, 0, T)
        csg = cs[:, :, g]
        cnt = (hi - lo).astype(jnp.float32)[None, :, None]
        means.append((csg[:, hi] - csg[:, lo]) / cnt)
    pooled = (jnp.stack(means, axis=2) - ug).astype(u.dtype)
    mixed = jnp.einsum('btgc,gcd->btgd', pooled, w_grp)
    return mixed.reshape(B, T, POOL_W) * scale


def token_shift(u, mu, reverse):
    if reverse:
        nb = jnp.pad(u[:, 1:], ((0, 0), (0, 1), (0, 0)))
    else:
        nb = jnp.pad(u[:, :-1], ((0, 0), (1, 0), (0, 0)))
    return u + (nb - u) * mu


def rwkv_dir_inputs(r0, k0, v0, wd, ad, mu_rkv, mu_lora, w0, w2, a0, a2, k_k, k_a, reverse):
    B, T, _ = r0.shape
    sh = lambda u, mu: token_shift(u.astype(jnp.float32), mu, reverse)
    hd = lambda u: u.reshape(B, T, RWKV_HEADS, RWKV_HEAD)
    r = sh(r0, mu_rkv[0])
    k = sh(k0, mu_rkv[1])
    v = sh(v0, mu_rkv[2])
    w_log = -jax.nn.softplus(-(w0 + jnp.tanh(sh(wd, mu_lora[0])) @ w2)) - 0.5
    decay = jnp.exp(-jnp.exp(w_log))
    a = jax.nn.sigmoid(a0 + sh(ad, mu_lora[1]) @ a2)
    kk = hd(k * k_k)
    kk = kk * lax.rsqrt(jnp.sum(jnp.square(kk), -1, keepdims=True) + 1e-12)
    k = k * (1.0 + (a - 1.0) * k_a)
    return (hd(r), hd(decay), hd(k), hd(v), -kk, kk * hd(a))


def wkv7_scan(state, inputs, reverse):
    def step(S, xs):
        r, w, k, v, a, b = xs
        sa = jnp.einsum('bhvk,bhk->bhv', S, a)
        S = S * w[:, :, None, :] + sa[..., None] * b[:, :, None, :] + v[..., None] * k[:, :, None, :]
        return S, jnp.einsum('bhvk,bhk->bhv', S, r)
    xs = tuple(jnp.swapaxes(u, 0, 1) for u in inputs)
    state, y = lax.scan(step, state, xs, reverse=reverse)
    return state, jnp.swapaxes(y, 0, 1)


def rwkv_readout(y, inputs, r_k, gn_g, gn_b):
    r, _, k, v, _, _ = inputs
    B, T = y.shape[:2]
    mu = jnp.mean(y, -1, keepdims=True)
    var = jnp.mean(jnp.square(y - mu), -1, keepdims=True)
    yn = ((y - mu) * lax.rsqrt(var + GN_EPS)).reshape(B, T, RWKV_W) * gn_g + gn_b
    bonus = jnp.sum(r * k * r_k, -1, keepdims=True) * v
    return yn + bonus.reshape(B, T, RWKV_W)


def split_even(p):
    a_in, r0, k0, v0, wd, ad, gate = jnp.split(p, EV_SPLITS, axis=-1)
    lead = p.shape[:-1]
    return a_in, r0, k0, v0, wd.reshape(*lead, 2, LORA_W), ad.reshape(*lead, 2, LORA_W), gate


def even_mixer(hl, hc, w_in, w_out, pool_w, pool_scale, mu_rkv, mu_lora, w0, w2, a0, a2,
               k_k, k_a, r_k, gn_g, gn_b, need_ctx_out):
    B = hl.shape[0]
    al, rl, kl, vl, wdl, adl, gl = split_even(hl @ w_in)
    ac, rc, kc, vc, wdc, adc, gc = split_even(hc @ w_in)
    outs_l, outs_c = [], []
    for d, reverse in enumerate((False, True)):
        dir_p = (mu_rkv[d], mu_lora[d], w0[d], w2[d], a0[d], a2[d], k_k, k_a)
        in_c = rwkv_dir_inputs(rc, kc, vc, wdc[..., d, :], adc[..., d, :], *dir_p, reverse)
        in_l = rwkv_dir_inputs(rl, kl, vl, wdl[..., d, :], adl[..., d, :], *dir_p, reverse)
        s0 = jnp.zeros((B, RWKV_HEADS, RWKV_HEAD, RWKV_HEAD), jnp.float32)
        s_ctx, y_c = wkv7_scan(s0, in_c, reverse)
        _, y_l = wkv7_scan(s_ctx, in_l, reverse)
        outs_l.append(rwkv_readout(y_l, in_l, r_k, gn_g[d], gn_b[d]))
        if need_ctx_out:
            outs_c.append(rwkv_readout(y_c, in_c, r_k, gn_g[d], gn_b[d]))
    ol = (outs_l[0] + outs_l[1]).astype(hl.dtype)
    yl = (jnp.concatenate([multiscale_pool(al, pool_w, pool_scale), ol], -1) * jax.nn.silu(gl)) @ w_out
    if not need_ctx_out:
        return yl, None
    oc = (outs_c[0] + outs_c[1]).astype(hc.dtype)
    yc = (jnp.concatenate([multiscale_pool(ac, pool_w, pool_scale), oc], -1) * jax.nn.silu(gc)) @ w_out
    return yl, yc


def odd_mixer(hl, hc, w_in, w_out, q_g, k_g, cos, sin, need_ctx_out):
    B, T, _ = hl.shape
    L = hc.shape[1]
    ql, kl, vl, gl = jnp.split(hl @ w_in, (ATT_W, ATT_W + KV_W, ATT_W + 2 * KV_W), axis=-1)
    ql = apply_axial_rope(rms_norm(ql.reshape(B, T, ATT_HEADS, HEAD_DIM), q_g), cos, sin)
    kl = apply_axial_rope(rms_norm(kl.reshape(B, T, KV_HEADS, HEAD_DIM), k_g), cos, sin)
    vl = vl.reshape(B, T, KV_HEADS, HEAD_DIM)
    kc, vc = jnp.split(hc @ w_in[:, ATT_W:ATT_W + 2 * KV_W], 2, axis=-1)
    kc = rms_norm(kc.reshape(B, L, KV_HEADS, HEAD_DIM), k_g)
    vc = vc.reshape(B, L, KV_HEADS, HEAD_DIM)
    k_all = jnp.concatenate([kc, kl], axis=1)
    v_all = jnp.concatenate([vc, vl], axis=1)
    ol = blocked_attention(ql.reshape(B, T, KV_HEADS, GROUP, HEAD_DIM), k_all, v_all)
    yl = (ol.reshape(B, T, ATT_W) * jax.nn.silu(gl)) @ w_out
    if not need_ctx_out:
        return yl, None
    qc = rms_norm((hc @ w_in[:, :ATT_W]).reshape(B, L, ATT_HEADS, HEAD_DIM), q_g)
    gc = hc @ w_in[:, ATT_W + 2 * KV_W:]
    oc = attention(qc.reshape(B, L, KV_HEADS, GROUP, HEAD_DIM), kc, vc)
    yc = (oc.reshape(B, L, ATT_W) * jax.nn.silu(gc)) @ w_out
    return yl, yc


def setup_inputs(seed: int = 0) -> dict:
    key = jax.random.key(seed)
    ks = iter(jax.random.split(key, 32))
    f32 = jnp.float32
    nrm = lambda shape, s=1.0: jax.random.normal(next(ks), shape, f32) * s
    uni = lambda shape: jax.random.uniform(next(ks), shape, f32)
    D = D_MODEL
    return {
        'x': nrm((BATCH, SEQ, D)),
        'c': nrm((BATCH, D)),
        'ctx': nrm((BATCH, CTX_LEN, D)),
        'c_ctx': nrm((D,)),
        'mod_w': nrm((DEPTH, D, 3 * D), 0.5 * D ** -0.5),
        'mod_b': nrm((DEPTH, 3 * D), 0.01),
        'ln_g': 1.0 + nrm((DEPTH, D), 0.05),
        'ln_b': nrm((DEPTH, D), 0.01),
        'ev_w_in': nrm((N_EVEN, D, EV_IN), D ** -0.5),
        'ev_w_out': nrm((N_EVEN, EV_GATE_W, D), DEEPNORM_BETA * EV_GATE_W ** -0.5),
        'pool_w': nrm((N_EVEN, N_POOL, POOL_GROUP, POOL_GROUP), POOL_GROUP ** -0.5),
        'pool_scale': 1.0 + nrm((N_EVEN, POOL_W), 0.1),
        'rw_mu_rkv': uni((N_EVEN, 2, 3, RWKV_W)),
        'rw_mu_lora': uni((N_EVEN, 2, 2, LORA_W)),
        'rw_w0': jnp.linspace(-6.0, 1.0, RWKV_W, dtype=f32) + nrm((N_EVEN, 2, RWKV_W), 0.1),
        'rw_w2': nrm((N_EVEN, 2, LORA_W, RWKV_W), 0.5 * LORA_W ** -0.5),
        'rw_a0': nrm((N_EVEN, 2, RWKV_W), 0.1),
        'rw_a2': nrm((N_EVEN, 2, LORA_W, RWKV_W), 0.5 * LORA_W ** -0.5),
        'rw_k_k': 0.85 + nrm((N_EVEN, RWKV_W), 0.05),
        'rw_k_a': 1.0 + nrm((N_EVEN, RWKV_W), 0.05),
        'rw_r_k': nrm((N_EVEN, RWKV_HEADS, RWKV_HEAD), 0.1),
        'rw_gn_g': 1.0 + nrm((N_EVEN, 2, RWKV_W), 0.05),
        'rw_gn_b': nrm((N_EVEN, 2, RWKV_W), 0.01),
        'od_w_in': nrm((N_ODD, D, OD_IN), D ** -0.5),
        'od_w_out': nrm((N_ODD, ATT_W, D), DEEPNORM_BETA * ATT_W ** -0.5),
        'q_norm_g': 1.0 + nrm((N_ODD, HEAD_DIM), 0.05),
        'k_norm_g': 1.0 + nrm((N_ODD, HEAD_DIM), 0.05),
    }


def reference(x, c, ctx, c_ctx, mod_w, mod_b, ln_g, ln_b, ev_w_in, ev_w_out, pool_w, pool_scale,
              rw_mu_rkv, rw_mu_lora, rw_w0, rw_w2, rw_a0, rw_a2, rw_k_k, rw_k_a, rw_r_k,
              rw_gn_g, rw_gn_b, od_w_in, od_w_out, q_norm_g, k_norm_g):
    ROWS = x.shape[1] // GRID_W
    cos, sin = axial_rope_tables(ROWS)
    sc = jax.nn.silu(c)
    scc = jax.nn.silu(c_ctx)
    xl, xc = x, ctx
    for l in range(DEPTH):
        last = l == DEPTH - 1
        i = l // 2
        shift_l, scale_l, gate_l = jnp.split((sc @ mod_w[l] + mod_b[l])[:, None, :], 3, axis=-1)
        shift_c, scale_c, gate_c = jnp.split(scc @ mod_w[l] + mod_b[l], 3, axis=-1)
        hl = xl * (1.0 + scale_l) + shift_l
        hc = xc * (1.0 + scale_c) + shift_c
        if l % 2 == 0:
            yl, yc = even_mixer(hl, hc, ev_w_in[i], ev_w_out[i], pool_w[i], pool_scale[i],
                                rw_mu_rkv[i], rw_mu_lora[i], rw_w0[i], rw_w2[i], rw_a0[i], rw_a2[i],
                                rw_k_k[i], rw_k_a[i], rw_r_k[i], rw_gn_g[i], rw_gn_b[i], not last)
        else:
            yl, yc = odd_mixer(hl, hc, od_w_in[i], od_w_out[i], q_norm_g[i], k_norm_g[i], cos, sin, not last)
        xl = layer_norm(DEEPNORM_ALPHA * xl + gate_l * yl, ln_g[l], ln_b[l])
        if not last:
            xc = layer_norm(DEEPNORM_ALPHA * xc + gate_c * yc, ln_g[l], ln_b[l])
    return xl
```

```python
import functools

import jax
import jax.numpy as jnp
from jax import lax
from jax.experimental import pallas as pl
from jax.experimental.pallas import tpu as pltpu

F32 = jnp.float32
BF16 = jnp.bfloat16

LANES = 128
VMEM_LIMIT = 52 * 1024 * 1024

GRID_W = 64
POOL_WINDOWS = (2, 4, 8, 16)
RWKV_HEAD = 64
LORA_W = 64
GN_EPS = 64e-5
HEAD_DIM = 128
GROUP = 4
ROPE_THETA = 10000.0
LN_EPS = 1e-6

MOD_ROWS = 24
TM = 1024
BLK = 256
CH = 64
HALO = 16


def _dot(a, b):
    return jnp.dot(a, b, preferred_element_type=F32)


def _dot_nt(a, b):
    return lax.dot_general(a, b, (((1,), (1,)), ((), ())), preferred_element_type=F32)


def _dot_tn(a, b):
    return lax.dot_general(a, b, (((0,), (0,)), ((), ())), preferred_element_type=F32)


def _cparams(sem):
    return pltpu.CompilerParams(dimension_semantics=sem, vmem_limit_bytes=VMEM_LIMIT)


def _mod_kernel(c_ref, w_ref, b_ref, o_ref):
    cc = c_ref[...]
    s = cc * jax.nn.sigmoid(cc)
    o_ref[0] = _dot(s.astype(BF16), w_ref[0].astype(BF16)) + b_ref[0]


def _modulation(c, c_ctx, mod_w, mod_b):
    depth, D, D3 = mod_w.shape
    B = c.shape[0]
    cc = jnp.zeros((MOD_ROWS, D), F32).at[:B].set(c).at[B].set(c_ctx)
    tn = 1024
    return pl.pallas_call(
        _mod_kernel,
        grid=(depth, D3 // tn),
        in_specs=[pl.BlockSpec((MOD_ROWS, D), lambda l, n: (0, 0)),
                  pl.BlockSpec((1, D, tn), lambda l, n: (l, 0, n)),
                  pl.BlockSpec((1, 1, tn), lambda l, n: (l, 0, n))],
        out_specs=pl.BlockSpec((1, MOD_ROWS, tn), lambda l, n: (l, 0, n)),
        out_shape=jax.ShapeDtypeStruct((depth, MOD_ROWS, D3), F32),
        compiler_params=_cparams(("arbitrary", "arbitrary")),
        name="modulation",
    )(cc, mod_w, mod_b.reshape(depth, 1, D3))


def _proj_kernel(x_ref, sh_ref, sc_ref, w_ref, o_ref, h_ref):
    @pl.when(pl.program_id(1) == 0)
    def _():
        h_ref[...] = (x_ref[...] * (1.0 + sc_ref[0]) + sh_ref[0]).astype(BF16)

    o_ref[...] = _dot(h_ref[...], w_ref[...]).astype(o_ref.dtype)


def _proj_tn(n):
    return next(t for t in (1280, 1024, 768, 512, 256) if n % t == 0)


def _projection(xa, mod3, w, B, T, tn):
    R, D = xa.shape
    N = w.shape[1]
    n_lat = (B * T) // TM
    per_b = T // TM

    def mrow(m):
        return jnp.where(m < n_lat, m // per_b, B)

    return pl.pallas_call(
        _proj_kernel,
        grid=(R // TM, N // tn),
        in_specs=[pl.BlockSpec((TM, D), lambda m, n: (m, 0)),
                  pl.BlockSpec((1, 1, D), lambda m, n: (mrow(m), 0, 0)),
                  pl.BlockSpec((1, 1, D), lambda m, n: (mrow(m), 0, 1)),
                  pl.BlockSpec((D, tn), lambda m, n: (0, n))],
        out_specs=pl.BlockSpec((TM, tn), lambda m, n: (m, n)),
        out_shape=jax.ShapeDtypeStruct((R, N), BF16),
        scratch_shapes=[pltpu.VMEM((TM, D), BF16)],
        compiler_params=_cparams(("arbitrary", "arbitrary")),
        name="in_proj",
    )(xa, mod3, mod3, w)


def _wkv_kernel(rev, n_ctx_ch,
                r0_ref, k0_ref, v0_ref, wd_ref, ad_ref, vec_ref, lvec_ref, w2_ref, a2_ref,
                bd_ref, tri_ref, cm_ref,
                o_ref, h_ref, cr_ref, ck_ref, cv_ref, cw_ref, ca_ref):
    i = pl.program_id(1)
    C = CH
    W = r0_ref.shape[-1]
    n_pairs = W // LANES

    @pl.when(i == 0)
    def _():
        h_ref[...] = jnp.zeros_like(h_ref)
        for c_ref in (cr_ref, ck_ref, cv_ref, cw_ref, ca_ref):
            c_ref[...] = jnp.zeros_like(c_ref)

    seg_start = jnp.logical_or(i == 0, i == n_ctx_ch)
    row = lax.broadcasted_iota(jnp.int32, (C, 1), 0)
    edge = row == (C - 1 if rev else 0)

    def tshift(x_ref, c_ref, mu):
        xv = x_ref[...].astype(F32)
        nb = pltpu.roll(xv, (C - 1) if rev else 1, axis=0)
        prev = jnp.where(seg_start, 0.0, c_ref[0:1, :])
        nb = jnp.where(edge, prev, nb)
        c_ref[0:1, :] = xv[0:1, :] if rev else xv[C - 1:C, :]
        return xv + (nb - xv) * mu

    def seg(xv):
        parts = [_dot(xv[:, g * 256:(g + 1) * 256].astype(BF16), bd_ref[...]) for g in range(W // 256)]
        return jnp.concatenate(parts, axis=1)

    vec = lambda j: vec_ref[j:j + 1, :]
    r = tshift(r0_ref, cr_ref, vec(0))
    k = tshift(k0_ref, ck_ref, vec(1))
    v = tshift(v0_ref, cv_ref, vec(2))
    wdl = jnp.tanh(tshift(wd_ref, cw_ref, lvec_ref[0:1, :]))
    w_pre = vec(3) + _dot(wdl.astype(BF16), w2_ref[...])
    z = -w_pre
    softplus = jnp.maximum(z, 0.0) + jnp.log(1.0 + jnp.exp(-jnp.abs(z)))
    lw = -jnp.exp(-softplus - 0.5)
    adl = tshift(ad_ref, ca_ref, lvec_ref[1:2, :])
    a = jax.nn.sigmoid(vec(4) + _dot(adl.astype(BF16), a2_ref[...]))
    kk0 = k * vec(5)
    kk = kk0 * lax.rsqrt(seg(kk0 * kk0) + 1e-12)
    k2 = k * (1.0 + (a - 1.0) * vec(6))
    b_in = kk * a
    bonus = seg(r * k2 * vec(7)) * v

    lw_hi = lw.astype(BF16)
    lw_lo = (lw - lw_hi.astype(F32)).astype(BF16)
    cum = _dot(tri_ref[...], lw_hi) + _dot(tri_ref[...], lw_lo)
    cend = cum[0:1, :] if rev else cum[C - 1:C, :]
    e_in = jnp.exp(cum)
    e_out = jnp.exp(-cum)
    g_end = jnp.exp(cend)
    at = -kk * jnp.exp(cum - lw)
    rt = r * e_in
    bt = b_in * e_out
    kt = k2 * e_out
    e_rem = e_out * g_end
    bh = b_in * e_rem
    kh = k2 * e_rem

    lane = lax.broadcasted_iota(jnp.int32, (1, LANES), 1)
    lo = lane < RWKV_HEAD
    lane2 = lax.broadcasted_iota(jnp.int32, (1, 2 * LANES), 1)
    lo2 = (lane2 % LANES) < RWKV_HEAD
    cm = cm_ref[...] > 0.5
    eye_c = (lax.broadcasted_iota(jnp.int32, (C, LANES), 0)
             == lax.broadcasted_iota(jnp.int32, (C, LANES), 1) % C).astype(F32)
    eye_p = (lax.broadcasted_iota(jnp.int32, (LANES, LANES), 0)
             == lax.broadcasted_iota(jnp.int32, (LANES, LANES), 1))
    zeros_c = jnp.zeros((C, LANES), F32)

    def first(xv, m=lo):
        return jnp.where(m, xv, 0.0)

    def second(xv, m=lo):
        return jnp.where(m, 0.0, xv)

    def bdiag(xv, m):
        return jnp.concatenate([first(xv, m), second(xv, m)], axis=0).astype(BF16)

    ys = []
    for p in range(n_pairs):
        sl = slice(p * LANES, (p + 1) * LANES)
        at_p, rt_p, bt_p, kt_p, bh_p, kh_p, v_p = (u[:, sl] for u in (at, rt, bt, kt, bh, kh, v))
        x0 = jnp.concatenate([first(at_p), first(rt_p)], axis=0).astype(BF16)
        x1 = jnp.concatenate([second(at_p), second(rt_p)], axis=0).astype(BF16)
        y0 = jnp.concatenate([bt_p, kt_p], axis=0).astype(BF16)
        y1 = jnp.concatenate([kt_p, bt_p], axis=0).astype(BF16)
        g0 = jnp.where(cm, _dot_nt(x0, y0), 0.0)
        g1 = jnp.where(cm, _dot_nt(x1, y1), 0.0)
        pcat = jnp.where(lo, g0[:C], g1[:C])
        aksw = jnp.where(lo, g1[:C], g0[:C])
        rbcat = jnp.where(lo, g0[C:], g1[C:])
        rksw = jnp.where(lo, g1[C:], g0[C:])
        v_sw = jnp.concatenate([second(v_p), first(v_p)], axis=0).astype(BF16)
        akv = _dot(aksw.astype(BF16), v_sw)

        tacc = eye_c + pcat
        pw = _dot(pcat.astype(BF16), bdiag(pcat, lo))
        n_sq = C.bit_length() - 1
        for j in range(1, n_sq):
            last = j == n_sq - 1
            if last:
                tacc = tacc + _dot(pw.astype(BF16), bdiag(tacc, lo))
            else:
                both = _dot(pw.astype(BF16), bdiag(jnp.concatenate([pw, tacc], axis=1), lo2))
                pw, tacc = both[:, :LANES], tacc + both[:, LANES:]
        av = _dot(tacc.astype(BF16), bdiag(jnp.concatenate([at_p, akv], axis=1), lo2))

        rhs = jnp.concatenate([
            first(av, lo2), second(av, lo2),
            jnp.concatenate([zeros_c, second(v_p)], axis=1),
            jnp.concatenate([zeros_c, first(v_p)], axis=1)], axis=0).astype(BF16)
        out_c = _dot(jnp.concatenate([rbcat, rksw], axis=1).astype(BF16), rhs)
        dm = jnp.concatenate([first(bh_p), second(bh_p), second(kh_p), first(kh_p)], axis=0).astype(BF16)
        out_d = _dot_tn(dm, rhs)
        r_new = rt_p + out_c[:, :LANES]
        m_bd = out_d[:, :LANES] + jnp.where(eye_p, g_end[:, sl], 0.0)

        h = h_ref[p]
        ob = _dot(jnp.concatenate([m_bd, r_new], axis=0).astype(BF16), h.astype(BF16))
        h_ref[p] = ob[:LANES] + out_d[:, LANES:]
        ys.append(ob[LANES:] + out_c[:, LANES:])

    y = jnp.concatenate(ys, axis=1)
    inv_n = 1.0 / RWKV_HEAD
    mu = seg(y) * inv_n
    d = y - mu
    var = seg(d * d) * inv_n
    o_ref[...] = d * lax.rsqrt(var + GN_EPS) * vec(8) + vec(9) + bonus


def _wkv_direction(P, rev, B, T, L, RW, off, vecs, lvec, w2p, a2p):
    R = P.shape[0]
    C = CH
    n_ctx, n_lat = L // C, T // C
    lat0, ctx0 = 0, (B * T) // C

    def chunk(b, i):
        if rev:
            return jnp.where(i < n_ctx, ctx0 + b * n_ctx + (n_ctx - 1 - i), lat0 + b * n_lat + (n_ctx + n_lat - 1 - i))
        return jnp.where(i < n_ctx, ctx0 + b * n_ctx + i, lat0 + b * n_lat + (i - n_ctx))

    t = jnp.arange(C)
    tri = (t[None, :] >= t[:, None]) if rev else (t[None, :] <= t[:, None])
    strict = (t[None, :] > t[:, None]) if rev else (t[None, :] < t[:, None])
    cm = jnp.concatenate([jnp.tile(strict, (1, 2)), jnp.tile(tri, (1, 2))], axis=0).astype(F32)
    g = jnp.arange(256) // RWKV_HEAD
    bd = (g[:, None] == g[None, :]).astype(BF16)
    n_pairs = RW // LANES
    col = lambda c0, w: (lambda b, i: (chunk(b, i), c0 // w))
    const = lambda b, i: (0, 0)
    return pl.pallas_call(
        functools.partial(_wkv_kernel, rev, n_ctx),
        grid=(B, n_ctx + n_lat),
        in_specs=[pl.BlockSpec((C, RW), col(off["r"], RW)),
                  pl.BlockSpec((C, RW), col(off["k"], RW)),
                  pl.BlockSpec((C, RW), col(off["v"], RW)),
                  pl.BlockSpec((C, LANES), col(off["wd"], LANES)),
                  pl.BlockSpec((C, LANES), col(off["ad"], LANES)),
                  pl.BlockSpec(vecs.shape, const),
                  pl.BlockSpec(lvec.shape, const),
                  pl.BlockSpec(w2p.shape, const),
                  pl.BlockSpec(a2p.shape, const),
                  pl.BlockSpec(bd.shape, const),
                  pl.BlockSpec((C, C), const),
                  pl.BlockSpec(cm.shape, const)],
        out_specs=pl.BlockSpec((C, RW), lambda b, i: (chunk(b, i), 0)),
        out_shape=jax.ShapeDtypeStruct((R, RW), F32),
        scratch_shapes=[pltpu.VMEM((n_pairs, LANES, LANES), F32),
                        pltpu.VMEM((8, RW), F32), pltpu.VMEM((8, RW), F32), pltpu.VMEM((8, RW), F32),
                        pltpu.VMEM((8, LANES), F32), pltpu.VMEM((8, LANES), F32)],
        compiler_params=_cparams(("arbitrary", "arbitrary")),
        name="wkv_rev" if rev else "wkv_fwd",
    )(P, P, P, P, P, vecs, lvec, w2p, a2p, bd, tri.astype(BF16), cm)


def _residual_ln(x, y, gate, g, b, alpha):
    zz = alpha * x + gate * y
    mu = jnp.mean(zz, axis=-1, keepdims=True)
    d = zz - mu
    var = jnp.mean(d * d, axis=-1, keepdims=True)
    return d * lax.rsqrt(var + LN_EPS) * g + b


def _silu(g):
    return g * jax.nn.sigmoid(g)


def _even_out_kernel(alpha, n_lat_blk, T, L,
                     x_ref, ac_ref, ap_ref, an_ref, gate_ref, yf_ref, yr_ref, mg_ref,
                     pw_ref, ps_ref, wo_ref, lg_ref, lb_ref, o_ref):
    m = pl.program_id(0)
    is_lat = m < n_lat_blk
    seg_len = jnp.where(is_lat, T, L)
    pos = jnp.where(is_lat, (m % (T // BLK)) * BLK, ((m - n_lat_blk) % (L // BLK)) * BLK)
    PW = ac_ref.shape[-1]
    G = PW // len(POOL_WINDOWS)

    t = pos + lax.broadcasted_iota(jnp.int32, (BLK, 1), 0)
    s_cur = pos + lax.broadcasted_iota(jnp.int32, (BLK, BLK), 1)
    s_prev = pos - HALO + lax.broadcasted_iota(jnp.int32, (BLK, HALO), 1)
    s_next = pos + BLK + lax.broadcasted_iota(jnp.int32, (BLK, HALO), 1)
    u_cur = ac_ref[...]
    u_prev = ap_ref[...]
    u_next = an_ref[...]
    mixed = []
    for gi, win in enumerate(POOL_WINDOWS):
        lo = jnp.clip(t - win // 2, 0, seg_len)
        hi = jnp.clip(t - win // 2 + win, 0, seg_len)
        inv = 1.0 / (hi - lo).astype(F32)

        def band(s):
            return jnp.where(s >= lo, jnp.where(s < hi, 1.0, 0.0), 0.0).astype(BF16)

        cs = slice(gi * G, (gi + 1) * G)
        sums = (_dot(band(s_cur), u_cur[:, cs]) + _dot(band(s_prev), u_prev[:, cs])
                + _dot(band(s_next), u_next[:, cs]))
        pooled = sums * inv - u_cur[:, cs].astype(F32)
        mixed.append(_dot(pooled.astype(BF16), pw_ref[gi]))
    a_pool = jnp.concatenate(mixed, axis=1) * ps_ref[...]
    act = jnp.concatenate([a_pool, yf_ref[...] + yr_ref[...]], axis=1) * _silu(gate_ref[...].astype(F32))
    y = _dot(act.astype(BF16), wo_ref[...])
    o_ref[...] = _residual_ln(x_ref[...], y, mg_ref[0], lg_ref[...], lb_ref[...], alpha)


def _odd_out_kernel(alpha, x_ref, att_ref, gate_ref, mg_ref, wo_ref, lg_ref, lb_ref, o_ref):
    act = att_ref[...].astype(F32) * _silu(gate_ref[...].astype(F32))
    y = _dot(act.astype(BF16), wo_ref[...])
    o_ref[...] = _residual_ln(x_ref[...], y, mg_ref[0], lg_ref[...], lb_ref[...], alpha)


def _mod_row(m, n_lat_blk, per_b, B):
    return jnp.where(m < n_lat_blk, m // per_b, B)


def _even_out(xa, P, yf, yr, mod3, pool_w, pool_scale, w_out, ln_g, ln_b, alpha, B, T, L, n_rows, off):
    R, D = xa.shape
    PW = pool_scale.shape[-1]
    n_lat_blk = (B * T) // BLK
    per_b = T // BLK
    hb = BLK // HALO
    last_halo = R // HALO - 1
    const2 = lambda m: (0, 0)
    return pl.pallas_call(
        functools.partial(_even_out_kernel, alpha, n_lat_blk, T, L),
        grid=(n_rows // BLK,),
        in_specs=[pl.BlockSpec((BLK, D), lambda m: (m, 0)),
                  pl.BlockSpec((BLK, PW), lambda m: (m, off["a"] // PW)),
                  pl.BlockSpec((HALO, PW), lambda m: (jnp.maximum(m * hb - 1, 0), off["a"] // PW)),
                  pl.BlockSpec((HALO, PW), lambda m: (jnp.minimum((m + 1) * hb, last_halo), off["a"] // PW)),
                  pl.BlockSpec((BLK, D), lambda m: (m, off["gate"] // D)),
                  pl.BlockSpec((BLK, D - PW), lambda m: (m, 0)),
                  pl.BlockSpec((BLK, D - PW), lambda m: (m, 0)),
                  pl.BlockSpec((1, 1, D), lambda m: (_mod_row(m, n_lat_blk, per_b, B), 0, 2)),
                  pl.BlockSpec(pool_w.shape, lambda m: (0, 0, 0)),
                  pl.BlockSpec((1, PW), const2),
                  pl.BlockSpec(w_out.shape, const2),
                  pl.BlockSpec((1, D), const2),
                  pl.BlockSpec((1, D), const2)],
        out_specs=pl.BlockSpec((BLK, D), lambda m: (m, 0)),
        out_shape=jax.ShapeDtypeStruct((n_rows, D), F32),
        compiler_params=_cparams(("arbitrary",)),
        name="even_out",
    )(xa, P, P, P, P, yf, yr, mod3, pool_w, pool_scale, w_out, ln_g, ln_b)


def _odd_out(xa, P, att, mod3, w_out, ln_g, ln_b, alpha, B, T, n_rows, off):
    R, D = xa.shape
    n_lat_blk = (B * T) // BLK
    per_b = T // BLK
    const2 = lambda m: (0, 0)
    return pl.pallas_call(
        functools.partial(_odd_out_kernel, alpha),
        grid=(n_rows // BLK,),
        in_specs=[pl.BlockSpec((BLK, D), lambda m: (m, 0)),
                  pl.BlockSpec((BLK, D), lambda m: (m, 0)),
                  pl.BlockSpec((BLK, D), lambda m: (m, off["gate"] // D)),
                  pl.BlockSpec((1, 1, D), lambda m: (_mod_row(m, n_lat_blk, per_b, B), 0, 2)),
                  pl.BlockSpec(w_out.shape, const2),
                  pl.BlockSpec((1, D), const2),
                  pl.BlockSpec((1, D), const2)],
        out_specs=pl.BlockSpec((BLK, D), lambda m: (m, 0)),
        out_shape=jax.ShapeDtypeStruct((n_rows, D), F32),
        compiler_params=_cparams(("arbitrary",)),
        name="odd_out",
    )(xa, att, P, mod3, w_out, ln_g, ln_b)


def _rope(xv, cos, sin):
    q = HEAD_DIM // 4
    lane = lax.broadcasted_iota(jnp.int32, (1, HEAD_DIM), 1)
    first_half = (lane % (2 * q)) < q
    partner = jnp.where(first_half, pltpu.roll(xv, HEAD_DIM - q, axis=1), pltpu.roll(xv, q, axis=1))
    return xv * cos + partner * sin


def _rms(xv, g):
    return xv * lax.rsqrt(jnp.mean(xv * xv, axis=-1, keepdims=True) + LN_EPS) * g


def _attn_kernel(has_ctx_q, L,
                 q_ref, kc_ref, kl_ref, vc_ref, vl_ref, cq_ref, sq_ref, cos_ref, sin_ref, qg_ref, kg_ref,
                 o_ref, ks_ref, vs_ref):
    i = pl.program_id(2)

    @pl.when(i == 0)
    def _():
        kf = jnp.concatenate([kc_ref[...], kl_ref[...]], axis=0).astype(F32)
        ks_ref[...] = _rope(_rms(kf, kg_ref[...]), cos_ref[...], sin_ref[...]).astype(BF16)
        vv = jnp.concatenate([vc_ref[...], vl_ref[...]], axis=0)
        vs_ref[...] = jnp.concatenate([vv, jnp.ones_like(vv)], axis=1)

    scale = HEAD_DIM ** -0.5
    qs = []
    for h in range(GROUP):
        qh = q_ref[:, h * HEAD_DIM:(h + 1) * HEAD_DIM].astype(F32)
        qh = _rope(_rms(qh, qg_ref[...]), cq_ref[...], sq_ref[...]) * scale
        qs.append(qh.astype(BF16))
    q4 = jnp.concatenate(qs, axis=0)

    def attend(nk):
        s = _dot_nt(q4, ks_ref[0:nk, :])
        mx = jnp.max(s, axis=-1, keepdims=True)
        pr = jnp.exp((s - mx).astype(BF16))
        o = _dot(pr, vs_ref[0:nk, :])
        o = o[:, :HEAD_DIM] / o[:, HEAD_DIM:]
        for h in range(GROUP):
            o_ref[:, h * HEAD_DIM:(h + 1) * HEAD_DIM] = o[h * BLK:(h + 1) * BLK].astype(o_ref.dtype)

    if has_ctx_q:
        @pl.when(i == 0)
        def _():
            attend(L)

        @pl.when(i > 0)
        def _():
            attend(ks_ref.shape[0])
    else:
        attend(ks_ref.shape[0])


def _attention(P, cos_t, sin_t, qg, kg, B, T, L, D, has_ctx_q, off):
    R = P.shape[0]
    KVH = D // HEAD_DIM // GROUP
    QW = GROUP * HEAD_DIM
    n_lat_blk = (B * T) // BLK
    per_b = T // BLK
    nq = per_b + (1 if has_ctx_q else 0)
    n_rows = R if has_ctx_q else B * T
    kcol = off["k"] // HEAD_DIM
    vcol = off["v"] // HEAD_DIM

    def qblk(b, i):
        if has_ctx_q:
            return jnp.where(i == 0, n_lat_blk + b, b * per_b + i - 1)
        return b * per_b + i

    tab = (lambda b, g, i: (i, 0)) if has_ctx_q else (lambda b, g, i: (i + 1, 0))
    const = lambda b, g, i: (0, 0)
    return pl.pallas_call(
        functools.partial(_attn_kernel, has_ctx_q, L),
        grid=(B, KVH, nq),
        in_specs=[pl.BlockSpec((BLK, QW), lambda b, g, i: (qblk(b, i), g)),
                  pl.BlockSpec((L, HEAD_DIM), lambda b, g, i: ((B * T) // L + b, kcol + g)),
                  pl.BlockSpec((T, HEAD_DIM), lambda b, g, i: (b, kcol + g)),
                  pl.BlockSpec((L, HEAD_DIM), lambda b, g, i: ((B * T) // L + b, vcol + g)),
                  pl.BlockSpec((T, HEAD_DIM), lambda b, g, i: (b, vcol + g)),
                  pl.BlockSpec((BLK, HEAD_DIM), tab),
                  pl.BlockSpec((BLK, HEAD_DIM), tab),
                  pl.BlockSpec((L + T, HEAD_DIM), const),
                  pl.BlockSpec((L + T, HEAD_DIM), const),
                  pl.BlockSpec((1, HEAD_DIM), const),
                  pl.BlockSpec((1, HEAD_DIM), const)],
        out_specs=pl.BlockSpec((BLK, QW), lambda b, g, i: (qblk(b, i), g)),
        out_shape=jax.ShapeDtypeStruct((n_rows, D), BF16),
        scratch_shapes=[pltpu.VMEM((L + T, HEAD_DIM), BF16), pltpu.VMEM((L + T, 2 * HEAD_DIM), BF16)],
        compiler_params=_cparams(("arbitrary", "arbitrary", "arbitrary")),
        name="attention",
    )(P, P, P, P, P, cos_t, sin_t, cos_t, sin_t, qg, kg)


def _rope_tables(T, L):
    q = HEAD_DIM // 4
    rows = T // GRID_W
    row = jnp.repeat(jnp.arange(rows, dtype=F32), GRID_W)
    col = jnp.tile(jnp.arange(GRID_W, dtype=F32), rows)
    inv_freq = ROPE_THETA ** (-jnp.arange(0, 2 * q, 2, dtype=F32) / (2 * q))
    ar, ac = row[:, None] * inv_freq, col[:, None] * inv_freq
    cos = jnp.concatenate([jnp.cos(ar), jnp.cos(ar), jnp.cos(ac), jnp.cos(ac)], axis=1)
    sin = jnp.concatenate([-jnp.sin(ar), jnp.sin(ar), -jnp.sin(ac), jnp.sin(ac)], axis=1)
    cos = jnp.concatenate([jnp.ones((L, HEAD_DIM), F32), cos], axis=0)
    sin = jnp.concatenate([jnp.zeros((L, HEAD_DIM), F32), sin], axis=0)
    return cos, sin


def _even_layer(xa, mod3, last, B, T, L, alpha, w_in, w_out, pool_w, pool_scale, mu_rkv, mu_lora, w0, w2,
                a0, a2, k_k, k_a, r_k, gn_g, gn_b, ln_g, ln_b):
    R, D = xa.shape
    PW = pool_scale.shape[-1]
    RW = k_k.shape[-1]
    n_split = PW + 3 * RW
    lora2 = 2 * LORA_W
    w = jnp.concatenate([w_in[:, :n_split], w_in[:, n_split + 2 * lora2:],
                         w_in[:, n_split:n_split + 2 * lora2]], axis=1).astype(BF16)
    off = {"a": 0, "r": PW, "k": PW + RW, "v": PW + 2 * RW, "gate": n_split,
           "wd": n_split + PW + RW, "ad": n_split + PW + RW + lora2}
    P = _projection(xa, mod3, w, B, T, _proj_tn(w.shape[1]))

    ys = []
    for d, rev in enumerate((False, True)):
        vecs = jnp.zeros((16, RW), F32)
        rows = [mu_rkv[d, 0], mu_rkv[d, 1], mu_rkv[d, 2], w0[d], a0[d], k_k, k_a, r_k.reshape(RW),
                gn_g[d], gn_b[d]]
        vecs = vecs.at[:len(rows)].set(jnp.stack(rows))
        lvec = jnp.zeros((8, lora2), F32).at[0].set(mu_lora[:, 0, :].reshape(lora2)).at[1].set(
            mu_lora[:, 1, :].reshape(lora2))
        w2p = jnp.zeros((lora2, RW), F32).at[d * LORA_W:(d + 1) * LORA_W].set(w2[d]).astype(BF16)
        a2p = jnp.zeros((lora2, RW), F32).at[d * LORA_W:(d + 1) * LORA_W].set(a2[d]).astype(BF16)
        ys.append(_wkv_direction(P, rev, B, T, L, RW, off, vecs, lvec, w2p, a2p))

    n_rows = B * T if last else R
    return _even_out(xa, P, ys[0], ys[1], mod3, pool_w.astype(BF16), pool_scale.reshape(1, PW),
                     w_out.astype(BF16), ln_g.reshape(1, D), ln_b.reshape(1, D), alpha, B, T, L, n_rows, off)


def _odd_layer(xa, mod3, last, B, T, L, alpha, w_in, w_out, q_g, k_g, tables, ln_g, ln_b):
    R, D = xa.shape
    KVW = D // GROUP
    w = jnp.concatenate([w_in[:, :D], w_in[:, D + 2 * KVW:], w_in[:, D:D + 2 * KVW]], axis=1).astype(BF16)
    off = {"q": 0, "gate": D, "k": 2 * D, "v": 2 * D + KVW}
    P = _projection(xa, mod3, w, B, T, _proj_tn(w.shape[1]))
    att = _attention(P, tables[0], tables[1], q_g.reshape(1, HEAD_DIM), k_g.reshape(1, HEAD_DIM),
                     B, T, L, D, not last, off)
    n_rows = B * T if last else R
    return _odd_out(xa, P, att, mod3, w_out.astype(BF16), ln_g.reshape(1, D), ln_b.reshape(1, D),
                    alpha, B, T, n_rows, off)


def kernel(x, c, ctx, c_ctx, mod_w, mod_b, ln_g, ln_b, ev_w_in, ev_w_out, pool_w, pool_scale, rw_mu_rkv,
           rw_mu_lora, rw_w0, rw_w2, rw_a0, rw_a2, rw_k_k, rw_k_a, rw_r_k, rw_gn_g, rw_gn_b, od_w_in,
           od_w_out, q_norm_g, k_norm_g):
    B, T, D = x.shape
    L = ctx.shape[1]
    depth = mod_w.shape[0]
    assert L == BLK and T % TM == 0 and (B * L) % TM == 0 and T % GRID_W == 0 and B < MOD_ROWS
    assert D // HEAD_DIM % GROUP == 0 and (D // 2) % 256 == 0
    alpha = (2 * depth) ** 0.25
    xa = jnp.concatenate([x.reshape(B * T, D), ctx.reshape(B * L, D)], axis=0)
    mod = _modulation(c, c_ctx, mod_w, mod_b)
    tables = _rope_tables(T, L)
    for l in range(depth):
        last = l == depth - 1
        i = l // 2
        mod3 = mod[l].reshape(MOD_ROWS, 1, 3 * D)
        if l % 2 == 0:
            xa = _even_layer(xa, mod3, last, B, T, L, alpha, ev_w_in[i], ev_w_out[i], pool_w[i], pool_scale[i],
                             rw_mu_rkv[i], rw_mu_lora[i], rw_w0[i], rw_w2[i], rw_a0[i], rw_a2[i], rw_k_k[i],
                             rw_k_a[i], rw_r_k[i], rw_gn_g[i], rw_gn_b[i], ln_g[l], ln_b[l])
        else:
            xa = _odd_layer(xa, mod3, last, B, T, L, alpha, od_w_in[i], od_w_out[i], q_norm_g[i], k_norm_g[i],
                            tables, ln_g[l], ln_b[l])
    return xa[:B * T].reshape(B, T, D)
```

```python
import functools

import jax
import jax.numpy as jnp
from jax import lax
from jax.experimental import pallas as pl
from jax.experimental.pallas import tpu as pltpu

F32 = jnp.float32
BF16 = jnp.bfloat16

LANES = 128
VMEM_LIMIT = 52 * 1024 * 1024

GRID_W = 64
POOL_WINDOWS = (2, 4, 8, 16)
RWKV_HEAD = 64
LORA_W = 64
GN_EPS = 64e-5
HEAD_DIM = 128
GROUP = 4
ROPE_THETA = 10000.0
LN_EPS = 1e-6
LOG2_E = 1.4426950408889634
MASKED = -1e30

MOD_ROWS = 24
TM = 1024
BLK = 256
CH = 64
WKV_ROWS = 256
HALO = 16


def _dot(a, b):
    return jnp.dot(a, b, preferred_element_type=F32)


def _dot_nt(a, b):
    return lax.dot_general(a, b, (((1,), (1,)), ((), ())), preferred_element_type=F32)


def _dot_tn(a, b):
    return lax.dot_general(a, b, (((0,), (0,)), ((), ())), preferred_element_type=F32)


def _cparams(sem):
    return pltpu.CompilerParams(dimension_semantics=sem, vmem_limit_bytes=VMEM_LIMIT)


def _mod_kernel(c_ref, w_ref, b_ref, o_ref):
    cc = c_ref[...]
    s = cc * jax.nn.sigmoid(cc)
    o_ref[0] = _dot(s.astype(BF16), w_ref[0].astype(BF16)) + b_ref[0]


def _modulation(c, c_ctx, mod_w, mod_b):
    depth, D, D3 = mod_w.shape
    B = c.shape[0]
    cc = jnp.zeros((MOD_ROWS, D), F32).at[:B].set(c).at[B].set(c_ctx)
    tn = 1024
    return pl.pallas_call(
        _mod_kernel,
        grid=(depth, D3 // tn),
        in_specs=[pl.BlockSpec((MOD_ROWS, D), lambda l, n: (0, 0)),
                  pl.BlockSpec((1, D, tn), lambda l, n: (l, 0, n)),
                  pl.BlockSpec((1, 1, tn), lambda l, n: (l, 0, n))],
        out_specs=pl.BlockSpec((1, MOD_ROWS, tn), lambda l, n: (l, 0, n)),
        out_shape=jax.ShapeDtypeStruct((depth, MOD_ROWS, D3), F32),
        compiler_params=_cparams(("arbitrary", "arbitrary")),
        name="modulation",
    )(cc, mod_w, mod_b.reshape(depth, 1, D3))


def _proj_kernel(x_ref, sh_ref, sc_ref, w_ref, o_ref, h_ref):
    @pl.when(pl.program_id(1) == 0)
    def _():
        h_ref[...] = (x_ref[...] * (1.0 + sc_ref[0]) + sh_ref[0]).astype(BF16)

    o_ref[...] = _dot(h_ref[...], w_ref[...]).astype(o_ref.dtype)


def _proj_tn(n):
    return next(t for t in (1280, 1024, 768, 512, 256) if n % t == 0)


def _projection(xa, mod3, w, B, T, tn):
    R, D = xa.shape
    N = w.shape[1]
    n_lat = (B * T) // TM
    per_b = T // TM

    def mrow(m):
        return jnp.where(m < n_lat, m // per_b, B)

    return pl.pallas_call(
        _proj_kernel,
        grid=(R // TM, N // tn),
        in_specs=[pl.BlockSpec((TM, D), lambda m, n: (m, 0)),
                  pl.BlockSpec((1, 1, D), lambda m, n: (mrow(m), 0, 0)),
                  pl.BlockSpec((1, 1, D), lambda m, n: (mrow(m), 0, 1)),
                  pl.BlockSpec((D, tn), lambda m, n: (0, n))],
        out_specs=pl.BlockSpec((TM, tn), lambda m, n: (m, n)),
        out_shape=jax.ShapeDtypeStruct((R, N), BF16),
        scratch_shapes=[pltpu.VMEM((TM, D), BF16)],
        compiler_params=_cparams(("arbitrary", "arbitrary")),
        name="in_proj",
    )(xa, mod3, mod3, w)


def _wkv_kernel(rev, n_ctx_blk,
                r0_ref, k0_ref, v0_ref, wd_ref, ad_ref, vec_ref, lvec_ref, w2_ref, a2_ref,
                bd_ref, tri_ref, cm_ref,
                o_ref, h_ref, cr_ref, ck_ref, cv_ref, cw_ref, ca_ref):
    i = pl.program_id(1)
    C = CH
    RB, W = r0_ref.shape
    n_pairs = W // LANES
    n_ch = RB // C

    @pl.when(i == 0)
    def _():
        h_ref[...] = jnp.zeros_like(h_ref)
        for c_ref in (cr_ref, ck_ref, cv_ref, cw_ref, ca_ref):
            c_ref[...] = jnp.zeros_like(c_ref)

    seg_start = jnp.logical_or(i == 0, i == n_ctx_blk)
    row = lax.broadcasted_iota(jnp.int32, (RB, 1), 0)
    edge = row == (RB - 1 if rev else 0)

    def tshift(x_ref, c_ref, mu):
        xv = x_ref[...].astype(F32)
        nb = pltpu.roll(xv, (RB - 1) if rev else 1, axis=0)
        prev = jnp.where(seg_start, 0.0, c_ref[0:1, :])
        nb = jnp.where(edge, prev, nb)
        c_ref[0:1, :] = xv[0:1, :] if rev else xv[RB - 1:RB, :]
        return xv + (nb - xv) * mu

    def seg(xv):
        parts = [_dot(xv[:, g * 256:(g + 1) * 256].astype(BF16), bd_ref[...]) for g in range(W // 256)]
        return jnp.concatenate(parts, axis=1)

    vec = lambda j: vec_ref[j:j + 1, :]
    r = tshift(r0_ref, cr_ref, vec(0))
    k = tshift(k0_ref, ck_ref, vec(1))
    v = tshift(v0_ref, cv_ref, vec(2))
    wdl = jnp.tanh(tshift(wd_ref, cw_ref, lvec_ref[0:1, :]))
    w_pre = vec(3) + _dot(wdl.astype(BF16), w2_ref[...])
    z = -w_pre
    softplus = jnp.maximum(z, 0.0) + jnp.log(1.0 + jnp.exp(-jnp.abs(z)))
    lw = -jnp.exp(-softplus - 0.5)
    adl = tshift(ad_ref, ca_ref, lvec_ref[1:2, :])
    a = jax.nn.sigmoid(vec(4) + _dot(adl.astype(BF16), a2_ref[...]))
    kk0 = k * vec(5)
    kk = kk0 * lax.rsqrt(seg(kk0 * kk0) + 1e-12)
    k2 = k * (1.0 + (a - 1.0) * vec(6))
    b_in = kk * a
    bonus = seg(r * k2 * vec(7)) * v

    lw_hi = lw.astype(BF16)
    lw_lo = (lw - lw_hi.astype(F32)).astype(BF16)
    cum = _dot(tri_ref[...], lw_hi) + _dot(tri_ref[...], lw_lo)
    e_in = jnp.exp(cum)
    e_out = jnp.exp(-cum)
    at = -kk * jnp.exp(cum - lw)
    rt = r * e_in
    bt = b_in * e_out
    kt = k2 * e_out
    chunks = range(n_ch)
    rows = [slice(c * C, (c + 1) * C) for c in chunks]
    last_row = [c * C if rev else (c + 1) * C - 1 for c in chunks]
    g_end = [jnp.exp(cum[last_row[c]:last_row[c] + 1, :]) for c in chunks]
    bh = [b_in[rows[c]] * (e_out[rows[c]] * g_end[c]) for c in chunks]
    kh = [k2[rows[c]] * (e_out[rows[c]] * g_end[c]) for c in chunks]

    lane = lax.broadcasted_iota(jnp.int32, (1, LANES), 1)
    lo = lane < RWKV_HEAD
    lane2 = lax.broadcasted_iota(jnp.int32, (1, 2 * LANES), 1)
    lo2 = (lane2 % LANES) < RWKV_HEAD
    cm = cm_ref[...] > 0.5
    eye_c = (lax.broadcasted_iota(jnp.int32, (C, LANES), 0)
             == lax.broadcasted_iota(jnp.int32, (C, LANES), 1) % C).astype(F32)
    eye_p = (lax.broadcasted_iota(jnp.int32, (LANES, LANES), 0)
             == lax.broadcasted_iota(jnp.int32, (LANES, LANES), 1))
    zeros_c = jnp.zeros((C, LANES), F32)

    def first(xv, m=lo):
        return jnp.where(m, xv, 0.0)

    def second(xv, m=lo):
        return jnp.where(m, 0.0, xv)

    def bdiag(xv, m):
        return jnp.concatenate([first(xv, m), second(xv, m)], axis=0).astype(BF16)

    pairs = range(n_pairs)
    units = [(c, p) for c in chunks for p in pairs]
    col = [slice(p * LANES, (p + 1) * LANES) for p in pairs]
    at_s, rt_s, bt_s, kt_s, v_s = ({(c, p): u[rows[c], col[p]] for c, p in units} for u in (at, rt, bt, kt, v))
    bh_s = {(c, p): bh[c][:, col[p]] for c, p in units}
    kh_s = {(c, p): kh[c][:, col[p]] for c, p in units}
    g0 = {u: jnp.where(cm, _dot_nt(jnp.concatenate([first(at_s[u]), first(rt_s[u])], axis=0).astype(BF16),
                                   jnp.concatenate([bt_s[u], kt_s[u]], axis=0).astype(BF16)), 0.0)
          for u in units}
    g1 = {u: jnp.where(cm, _dot_nt(jnp.concatenate([second(at_s[u]), second(rt_s[u])], axis=0).astype(BF16),
                                   jnp.concatenate([kt_s[u], bt_s[u]], axis=0).astype(BF16)), 0.0)
          for u in units}
    pcat = {u: jnp.where(lo, g0[u][:C], g1[u][:C]) for u in units}
    aksw = {u: jnp.where(lo, g1[u][:C], g0[u][:C]) for u in units}
    rbcat = {u: jnp.where(lo, g0[u][C:], g1[u][C:]) for u in units}
    rksw = {u: jnp.where(lo, g1[u][C:], g0[u][C:]) for u in units}
    akv = {u: _dot(aksw[u].astype(BF16),
                   jnp.concatenate([second(v_s[u]), first(v_s[u])], axis=0).astype(BF16))
           for u in units}

    tacc = {u: eye_c + pcat[u] for u in units}
    pw = {u: _dot(pcat[u].astype(BF16), bdiag(pcat[u], lo)) for u in units}
    n_sq = C.bit_length() - 1
    for j in range(1, n_sq):
        if j == n_sq - 1:
            tacc = {u: tacc[u] + _dot(pw[u].astype(BF16), bdiag(tacc[u], lo)) for u in units}
        else:
            both = {u: _dot(pw[u].astype(BF16), bdiag(jnp.concatenate([pw[u], tacc[u]], axis=1), lo2))
                    for u in units}
            pw = {u: both[u][:, :LANES] for u in units}
            tacc = {u: tacc[u] + both[u][:, LANES:] for u in units}
    av = {u: _dot(tacc[u].astype(BF16), bdiag(jnp.concatenate([at_s[u], akv[u]], axis=1), lo2))
          for u in units}

    rhs = {u: jnp.concatenate([
        first(av[u], lo2), second(av[u], lo2),
        jnp.concatenate([zeros_c, second(v_s[u])], axis=1),
        jnp.concatenate([zeros_c, first(v_s[u])], axis=1)], axis=0).astype(BF16) for u in units}
    out_c = {u: _dot(jnp.concatenate([rbcat[u], rksw[u]], axis=1).astype(BF16), rhs[u])
             for u in units}
    out_d = {u: _dot_tn(jnp.concatenate([first(bh_s[u]), second(bh_s[u]), second(kh_s[u]), first(kh_s[u])],
                                        axis=0).astype(BF16), rhs[u]) for u in units}
    lhs_b = {(c, p): jnp.concatenate([out_d[c, p][:, :LANES] + jnp.where(eye_p, g_end[c][:, col[p]], 0.0),
                                      rt_s[c, p] + out_c[c, p][:, :LANES]], axis=0).astype(BF16)
             for c, p in units}

    h = [h_ref[p] for p in pairs]
    y_rows = [None] * n_ch
    for c in (reversed(chunks) if rev else chunks):
        ob = [_dot(lhs_b[c, p], h[p].astype(BF16)) for p in pairs]
        h = [ob[p][:LANES] + out_d[c, p][:, LANES:] for p in pairs]
        y_rows[c] = jnp.concatenate([ob[p][LANES:] + out_c[c, p][:, LANES:] for p in pairs], axis=1)
    for p in pairs:
        h_ref[p] = h[p]
    y = jnp.concatenate(y_rows, axis=0)
    inv_n = 1.0 / RWKV_HEAD
    mu = seg(y) * inv_n
    d = y - mu
    var = seg(d * d) * inv_n
    o_ref[...] = d * lax.rsqrt(var + GN_EPS) * vec(8) + vec(9) + bonus


def _wkv_direction(P, rev, B, T, L, RW, off, vecs, lvec, w2p, a2p):
    R = P.shape[0]
    C = CH
    RB = WKV_ROWS
    n_ctx, n_lat = L // RB, T // RB
    lat0, ctx0 = 0, (B * T) // RB

    def chunk(b, i):
        if rev:
            return jnp.where(i < n_ctx, ctx0 + b * n_ctx + (n_ctx - 1 - i), lat0 + b * n_lat + (n_ctx + n_lat - 1 - i))
        return jnp.where(i < n_ctx, ctx0 + b * n_ctx + i, lat0 + b * n_lat + (i - n_ctx))

    t = jnp.arange(C)
    tri = (t[None, :] >= t[:, None]) if rev else (t[None, :] <= t[:, None])
    strict = (t[None, :] > t[:, None]) if rev else (t[None, :] < t[:, None])
    cm = jnp.concatenate([jnp.tile(strict, (1, 2)), jnp.tile(tri, (1, 2))], axis=0).astype(F32)
    tri = jnp.kron(jnp.eye(RB // C, dtype=F32), tri.astype(F32))
    g = jnp.arange(256) // RWKV_HEAD
    bd = (g[:, None] == g[None, :]).astype(BF16)
    n_pairs = RW // LANES
    col = lambda c0, w: (lambda b, i: (chunk(b, i), c0 // w))
    const = lambda b, i: (0, 0)
    return pl.pallas_call(
        functools.partial(_wkv_kernel, rev, n_ctx),
        grid=(B, n_ctx + n_lat),
        in_specs=[pl.BlockSpec((RB, RW), col(off["r"], RW)),
                  pl.BlockSpec((RB, RW), col(off["k"], RW)),
                  pl.BlockSpec((RB, RW), col(off["v"], RW)),
                  pl.BlockSpec((RB, LANES), col(off["wd"], LANES)),
                  pl.BlockSpec((RB, LANES), col(off["ad"], LANES)),
                  pl.BlockSpec(vecs.shape, const),
                  pl.BlockSpec(lvec.shape, const),
                  pl.BlockSpec(w2p.shape, const),
                  pl.BlockSpec(a2p.shape, const),
                  pl.BlockSpec(bd.shape, const),
                  pl.BlockSpec((RB, RB), const),
                  pl.BlockSpec(cm.shape, const)],
        out_specs=pl.BlockSpec((RB, RW), lambda b, i: (chunk(b, i), 0)),
        out_shape=jax.ShapeDtypeStruct((R, RW), F32),
        scratch_shapes=[pltpu.VMEM((n_pairs, LANES, LANES), F32),
                        pltpu.VMEM((8, RW), F32), pltpu.VMEM((8, RW), F32), pltpu.VMEM((8, RW), F32),
                        pltpu.VMEM((8, LANES), F32), pltpu.VMEM((8, LANES), F32)],
        compiler_params=_cparams(("arbitrary", "arbitrary")),
        name="wkv_rev" if rev else "wkv_fwd",
    )(P, P, P, P, P, vecs, lvec, w2p, a2p, bd, tri.astype(BF16), cm)


def _residual_ln(x, y, gate, g, b, alpha):
    zz = alpha * x + gate * y
    mu = jnp.mean(zz, axis=-1, keepdims=True)
    d = zz - mu
    var = jnp.mean(d * d, axis=-1, keepdims=True)
    return d * lax.rsqrt(var + LN_EPS) * g + b


def _silu(g):
    return g * jax.nn.sigmoid(g)


def _even_out_kernel(alpha, n_lat_blk, T, L,
                     x_ref, ac_ref, ap_ref, an_ref, gate_ref, yf_ref, yr_ref, mg_ref,
                     pw_ref, ps_ref, wo_ref, lg_ref, lb_ref, o_ref):
    m = pl.program_id(0)
    is_lat = m < n_lat_blk
    seg_len = jnp.where(is_lat, T, L)
    pos = jnp.where(is_lat, (m % (T // BLK)) * BLK, ((m - n_lat_blk) % (L // BLK)) * BLK)
    PW = ac_ref.shape[-1]
    G = PW // len(POOL_WINDOWS)

    t = pos + lax.broadcasted_iota(jnp.int32, (BLK, 1), 0)
    s_cur = pos + lax.broadcasted_iota(jnp.int32, (BLK, BLK), 1)
    s_prev = pos - HALO + lax.broadcasted_iota(jnp.int32, (BLK, HALO), 1)
    s_next = pos + BLK + lax.broadcasted_iota(jnp.int32, (BLK, HALO), 1)
    u_cur = ac_ref[...]
    u_prev = ap_ref[...]
    u_next = an_ref[...]
    mixed = []
    for gi, win in enumerate(POOL_WINDOWS):
        lo = jnp.clip(t - win // 2, 0, seg_len)
        hi = jnp.clip(t - win // 2 + win, 0, seg_len)
        inv = 1.0 / (hi - lo).astype(F32)

        def band(s):
            return jnp.where(s >= lo, jnp.where(s < hi, 1.0, 0.0), 0.0).astype(BF16)

        cs = slice(gi * G, (gi + 1) * G)
        sums = (_dot(band(s_cur), u_cur[:, cs]) + _dot(band(s_prev), u_prev[:, cs])
                + _dot(band(s_next), u_next[:, cs]))
        pooled = sums * inv - u_cur[:, cs].astype(F32)
        mixed.append(_dot(pooled.astype(BF16), pw_ref[gi]))
    a_pool = jnp.concatenate(mixed, axis=1) * ps_ref[...]
    act = jnp.concatenate([a_pool, yf_ref[...] + yr_ref[...]], axis=1) * _silu(gate_ref[...].astype(F32))
    y = _dot(act.astype(BF16), wo_ref[...])
    o_ref[...] = _residual_ln(x_ref[...], y, mg_ref[0], lg_ref[...], lb_ref[...], alpha)


def _odd_out_kernel(alpha, x_ref, att_ref, gate_ref, mg_ref, wo_ref, lg_ref, lb_ref, o_ref):
    act = att_ref[...].astype(F32) * _silu(gate_ref[...].astype(F32))
    y = _dot(act.astype(BF16), wo_ref[...])
    o_ref[...] = _residual_ln(x_ref[...], y, mg_ref[0], lg_ref[...], lb_ref[...], alpha)


def _mod_row(m, n_lat_blk, per_b, B):
    return jnp.where(m < n_lat_blk, m // per_b, B)


def _even_out(xa, P, yf, yr, mod3, pool_w, pool_scale, w_out, ln_g, ln_b, alpha, B, T, L, n_rows, off):
    R, D = xa.shape
    PW = pool_scale.shape[-1]
    n_lat_blk = (B * T) // BLK
    per_b = T // BLK
    hb = BLK // HALO
    last_halo = R // HALO - 1
    const2 = lambda m: (0, 0)
    return pl.pallas_call(
        functools.partial(_even_out_kernel, alpha, n_lat_blk, T, L),
        grid=(n_rows // BLK,),
        in_specs=[pl.BlockSpec((BLK, D), lambda m: (m, 0)),
                  pl.BlockSpec((BLK, PW), lambda m: (m, off["a"] // PW)),
                  pl.BlockSpec((HALO, PW), lambda m: (jnp.maximum(m * hb - 1, 0), off["a"] // PW)),
                  pl.BlockSpec((HALO, PW), lambda m: (jnp.minimum((m + 1) * hb, last_halo), off["a"] // PW)),
                  pl.BlockSpec((BLK, D), lambda m: (m, off["gate"] // D)),
                  pl.BlockSpec((BLK, D - PW), lambda m: (m, 0)),
                  pl.BlockSpec((BLK, D - PW), lambda m: (m, 0)),
                  pl.BlockSpec((1, 1, D), lambda m: (_mod_row(m, n_lat_blk, per_b, B), 0, 2)),
                  pl.BlockSpec(pool_w.shape, lambda m: (0, 0, 0)),
                  pl.BlockSpec((1, PW), const2),
                  pl.BlockSpec(w_out.shape, const2),
                  pl.BlockSpec((1, D), const2),
                  pl.BlockSpec((1, D), const2)],
        out_specs=pl.BlockSpec((BLK, D), lambda m: (m, 0)),
        out_shape=jax.ShapeDtypeStruct((n_rows, D), F32),
        compiler_params=_cparams(("arbitrary",)),
        name="even_out",
    )(xa, P, P, P, P, yf, yr, mod3, pool_w, pool_scale, w_out, ln_g, ln_b)


def _odd_out(xa, P, att, mod3, w_out, ln_g, ln_b, alpha, B, T, n_rows, off):
    R, D = xa.shape
    n_lat_blk = (B * T) // BLK
    per_b = T // BLK
    const2 = lambda m: (0, 0)
    return pl.pallas_call(
        functools.partial(_odd_out_kernel, alpha),
        grid=(n_rows // BLK,),
        in_specs=[pl.BlockSpec((BLK, D), lambda m: (m, 0)),
                  pl.BlockSpec((BLK, D), lambda m: (m, 0)),
                  pl.BlockSpec((BLK, D), lambda m: (m, off["gate"] // D)),
                  pl.BlockSpec((1, 1, D), lambda m: (_mod_row(m, n_lat_blk, per_b, B), 0, 2)),
                  pl.BlockSpec(w_out.shape, const2),
                  pl.BlockSpec((1, D), const2),
                  pl.BlockSpec((1, D), const2)],
        out_specs=pl.BlockSpec((BLK, D), lambda m: (m, 0)),
        out_shape=jax.ShapeDtypeStruct((n_rows, D), F32),
        compiler_params=_cparams(("arbitrary",)),
        name="odd_out",
    )(xa, att, P, mod3, w_out, ln_g, ln_b)


def _rope(xv, cos, sin):
    q = HEAD_DIM // 4
    lane = lax.broadcasted_iota(jnp.int32, (1, HEAD_DIM), 1)
    first_half = (lane % (2 * q)) < q
    partner = jnp.where(first_half, pltpu.roll(xv, HEAD_DIM - q, axis=1), pltpu.roll(xv, q, axis=1))
    return xv * cos + partner * sin


def _rms(xv, g):
    return xv * lax.rsqrt(jnp.mean(xv * xv, axis=-1, keepdims=True) + LN_EPS) * g


def _attn_kernel(has_ctx_q, L,
                 q_ref, kc_ref, kl_ref, vc_ref, vl_ref, cq_ref, sq_ref, cos_ref, sin_ref, qg_ref, kg_ref,
                 o_ref, ks_ref, vs_ref):
    i = pl.program_id(2)

    @pl.when(i == 0)
    def _():
        kf = jnp.concatenate([kc_ref[...], kl_ref[...]], axis=0).astype(F32)
        ks_ref[...] = _rope(_rms(kf, kg_ref[...]), cos_ref[...], sin_ref[...]).astype(BF16)
        vv = jnp.concatenate([vc_ref[...], vl_ref[...]], axis=0)
        vs_ref[...] = jnp.concatenate([vv, jnp.ones_like(vv)], axis=1)

    scale = HEAD_DIM ** -0.5 * LOG2_E
    nk = ks_ref.shape[0]
    if has_ctx_q:
        key = lax.broadcasted_iota(jnp.int32, (1, nk), 1)
        bias = jnp.where(jnp.logical_and(i == 0, key >= L), MASKED, 0.0)

    def scores(h):
        qh = q_ref[:, h * HEAD_DIM:(h + 1) * HEAD_DIM].astype(F32)
        qh = _rope(_rms(qh, qg_ref[...]), cq_ref[...], sq_ref[...]) * scale
        s = _dot_nt(qh.astype(BF16), ks_ref[...])
        return s + bias if has_ctx_q else s

    def weighted(s):
        mx = jnp.max(s, axis=-1, keepdims=True)
        pr = jnp.exp2((s - mx).astype(BF16))
        o = _dot(pr, vs_ref[...])
        return o[:, :HEAD_DIM] / o[:, HEAD_DIM:]

    s_next = scores(0)
    for h in range(GROUP):
        s_cur = s_next
        if h + 1 < GROUP:
            s_next = scores(h + 1)
        o_ref[:, h * HEAD_DIM:(h + 1) * HEAD_DIM] = weighted(s_cur).astype(o_ref.dtype)


def _attention(P, cos_t, sin_t, qg, kg, B, T, L, D, has_ctx_q, off):
    R = P.shape[0]
    KVH = D // HEAD_DIM // GROUP
    QW = GROUP * HEAD_DIM
    n_lat_blk = (B * T) // BLK
    per_b = T // BLK
    nq = per_b + (1 if has_ctx_q else 0)
    n_rows = R if has_ctx_q else B * T
    kcol = off["k"] // HEAD_DIM
    vcol = off["v"] // HEAD_DIM

    def qblk(b, i):
        if has_ctx_q:
            return jnp.where(i == 0, n_lat_blk + b, b * per_b + i - 1)
        return b * per_b + i

    tab = (lambda b, g, i: (i, 0)) if has_ctx_q else (lambda b, g, i: (i + 1, 0))
    const = lambda b, g, i: (0, 0)
    return pl.pallas_call(
        functools.partial(_attn_kernel, has_ctx_q, L),
        grid=(B, KVH, nq),
        in_specs=[pl.BlockSpec((BLK, QW), lambda b, g, i: (qblk(b, i), g)),
                  pl.BlockSpec((L, HEAD_DIM), lambda b, g, i: ((B * T) // L + b, kcol + g)),
                  pl.BlockSpec((T, HEAD_DIM), lambda b, g, i: (b, kcol + g)),
                  pl.BlockSpec((L, HEAD_DIM), lambda b, g, i: ((B * T) // L + b, vcol + g)),
                  pl.BlockSpec((T, HEAD_DIM), lambda b, g, i: (b, vcol + g)),
                  pl.BlockSpec((BLK, HEAD_DIM), tab),
                  pl.BlockSpec((BLK, HEAD_DIM), tab),
                  pl.BlockSpec((L + T, HEAD_DIM), const),
                  pl.BlockSpec((L + T, HEAD_DIM), const),
                  pl.BlockSpec((1, HEAD_DIM), const),
                  pl.BlockSpec((1, HEAD_DIM), const)],
        out_specs=pl.BlockSpec((BLK, QW), lambda b, g, i: (qblk(b, i), g)),
        out_shape=jax.ShapeDtypeStruct((n_rows, D), BF16),
        scratch_shapes=[pltpu.VMEM((L + T, HEAD_DIM), BF16), pltpu.VMEM((L + T, 2 * HEAD_DIM), BF16)],
        compiler_params=_cparams(("arbitrary", "arbitrary", "arbitrary")),
        name="attention",
    )(P, P, P, P, P, cos_t, sin_t, cos_t, sin_t, qg, kg)


def _rope_tables(T, L):
    q = HEAD_DIM // 4
    rows = T // GRID_W
    row = jnp.repeat(jnp.arange(rows, dtype=F32), GRID_W)
    col = jnp.tile(jnp.arange(GRID_W, dtype=F32), rows)
    inv_freq = ROPE_THETA ** (-jnp.arange(0, 2 * q, 2, dtype=F32) / (2 * q))
    ar, ac = row[:, None] * inv_freq, col[:, None] * inv_freq
    cos = jnp.concatenate([jnp.cos(ar), jnp.cos(ar), jnp.cos(ac), jnp.cos(ac)], axis=1)
    sin = jnp.concatenate([-jnp.sin(ar), jnp.sin(ar), -jnp.sin(ac), jnp.sin(ac)], axis=1)
    cos = jnp.concatenate([jnp.ones((L, HEAD_DIM), F32), cos], axis=0)
    sin = jnp.concatenate([jnp.zeros((L, HEAD_DIM), F32), sin], axis=0)
    return cos, sin


def _even_layer(xa, mod3, last, B, T, L, alpha, w_in, w_out, pool_w, pool_scale, mu_rkv, mu_lora, w0, w2,
                a0, a2, k_k, k_a, r_k, gn_g, gn_b, ln_g, ln_b):
    R, D = xa.shape
    PW = pool_scale.shape[-1]
    RW = k_k.shape[-1]
    n_split = PW + 3 * RW
    lora2 = 2 * LORA_W
    w = jnp.concatenate([w_in[:, :n_split], w_in[:, n_split + 2 * lora2:],
                         w_in[:, n_split:n_split + 2 * lora2]], axis=1).astype(BF16)
    off = {"a": 0, "r": PW, "k": PW + RW, "v": PW + 2 * RW, "gate": n_split,
           "wd": n_split + PW + RW, "ad": n_split + PW + RW + lora2}
    P = _projection(xa, mod3, w, B, T, _proj_tn(w.shape[1]))

    ys = []
    for d, rev in enumerate((False, True)):
        vecs = jnp.zeros((16, RW), F32)
        rows = [mu_rkv[d, 0], mu_rkv[d, 1], mu_rkv[d, 2], w0[d], a0[d], k_k, k_a, r_k.reshape(RW),
                gn_g[d], gn_b[d]]
        vecs = vecs.at[:len(rows)].set(jnp.stack(rows))
        lvec = jnp.zeros((8, lora2), F32).at[0].set(mu_lora[:, 0, :].reshape(lora2)).at[1].set(
            mu_lora[:, 1, :].reshape(lora2))
        w2p = jnp.zeros((lora2, RW), F32).at[d * LORA_W:(d + 1) * LORA_W].set(w2[d]).astype(BF16)
        a2p = jnp.zeros((lora2, RW), F32).at[d * LORA_W:(d + 1) * LORA_W].set(a2[d]).astype(BF16)
        ys.append(_wkv_direction(P, rev, B, T, L, RW, off, vecs, lvec, w2p, a2p))

    n_rows = B * T if last else R
    return _even_out(xa, P, ys[0], ys[1], mod3, pool_w.astype(BF16), pool_scale.reshape(1, PW),
                     w_out.astype(BF16), ln_g.reshape(1, D), ln_b.reshape(1, D), alpha, B, T, L, n_rows, off)


def _odd_layer(xa, mod3, last, B, T, L, alpha, w_in, w_out, q_g, k_g, tables, ln_g, ln_b):
    R, D = xa.shape
    KVW = D // GROUP
    w = jnp.concatenate([w_in[:, :D], w_in[:, D + 2 * KVW:], w_in[:, D:D + 2 * KVW]], axis=1).astype(BF16)
    off = {"q": 0, "gate": D, "k": 2 * D, "v": 2 * D + KVW}
    P = _projection(xa, mod3, w, B, T, _proj_tn(w.shape[1]))
    att = _attention(P, tables[0], tables[1], q_g.reshape(1, HEAD_DIM), k_g.reshape(1, HEAD_DIM),
                     B, T, L, D, not last, off)
    n_rows = B * T if last else R
    return _odd_out(xa, P, att, mod3, w_out.astype(BF16), ln_g.reshape(1, D), ln_b.reshape(1, D),
                    alpha, B, T, n_rows, off)


def kernel(x, c, ctx, c_ctx, mod_w, mod_b, ln_g, ln_b, ev_w_in, ev_w_out, pool_w, pool_scale, rw_mu_rkv,
           rw_mu_lora, rw_w0, rw_w2, rw_a0, rw_a2, rw_k_k, rw_k_a, rw_r_k, rw_gn_g, rw_gn_b, od_w_in,
           od_w_out, q_norm_g, k_norm_g):
    B, T, D = x.shape
    L = ctx.shape[1]
    depth = mod_w.shape[0]
    assert L == BLK and T % TM == 0 and (B * L) % TM == 0 and T % GRID_W == 0 and B < MOD_ROWS
    assert D // HEAD_DIM % GROUP == 0 and (D // 2) % 256 == 0
    alpha = (2 * depth) ** 0.25
    xa = jnp.concatenate([x.reshape(B * T, D), ctx.reshape(B * L, D)], axis=0)
    mod = _modulation(c, c_ctx, mod_w, mod_b)
    tables = _rope_tables(T, L)
    for l in range(depth):
        last = l == depth - 1
        i = l // 2
        mod3 = mod[l].reshape(MOD_ROWS, 1, 3 * D)
        if l % 2 == 0:
            xa = _even_layer(xa, mod3, last, B, T, L, alpha, ev_w_in[i], ev_w_out[i], pool_w[i], pool_scale[i],
                             rw_mu_rkv[i], rw_mu_lora[i], rw_w0[i], rw_w2[i], rw_a0[i], rw_a2[i], rw_k_k[i],
                             rw_k_a[i], rw_r_k[i], rw_gn_g[i], rw_gn_b[i], ln_g[l], ln_b[l])
        else:
            xa = _odd_layer(xa, mod3, last, B, T, L, alpha, od_w_in[i], od_w_out[i], q_norm_g[i], k_norm_g[i],
                            tables, ln_g[l], ln_b[l])
    return xa[:B * T].reshape(B, T, D)
```

```python
import functools

import jax
import jax.numpy as jnp
from jax import lax
from jax.experimental import pallas as pl
from jax.experimental.pallas import tpu as pltpu

F32 = jnp.float32
BF16 = jnp.bfloat16

LANES = 128
VMEM_LIMIT = 52 * 1024 * 1024

GRID_W = 64
POOL_WINDOWS = (2, 4, 8, 16)
RWKV_HEAD = 64
LORA_W = 64
GN_EPS = 64e-5
HEAD_DIM = 128
GROUP = 4
ROPE_THETA = 10000.0
LN_EPS = 1e-6
LOG2_E = 1.4426950408889634
EXP_NEG_HALF = 0.6065306597126334

MOD_ROWS = 24
TM = 1024
BLK = 256
ROWS2 = 2 * BLK
CH = 64
WKV_ROWS = 256
HALO = 16


def _dot(a, b):
    return jnp.dot(a, b, preferred_element_type=F32)


def _dot_nt(a, b):
    return lax.dot_general(a, b, (((1,), (1,)), ((), ())), preferred_element_type=F32)


def _dot_tn(a, b):
    return lax.dot_general(a, b, (((0,), (0,)), ((), ())), preferred_element_type=F32)


def _cparams(sem):
    return pltpu.CompilerParams(dimension_semantics=sem, vmem_limit_bytes=VMEM_LIMIT)


def _mod_kernel(c_ref, w_ref, b_ref, o_ref):
    cc = c_ref[...]
    s = cc * jax.nn.sigmoid(cc)
    o_ref[0] = _dot(s.astype(BF16), w_ref[0].astype(BF16)) + b_ref[0]


def _modulation(c, c_ctx, mod_w, mod_b):
    depth, D, D3 = mod_w.shape
    B = c.shape[0]
    cc = jnp.zeros((MOD_ROWS, D), F32).at[:B].set(c).at[B].set(c_ctx)
    tn = 1024
    return pl.pallas_call(
        _mod_kernel,
        grid=(depth, D3 // tn),
        in_specs=[pl.BlockSpec((MOD_ROWS, D), lambda l, n: (0, 0)),
                  pl.BlockSpec((1, D, tn), lambda l, n: (l, 0, n)),
                  pl.BlockSpec((1, 1, tn), lambda l, n: (l, 0, n))],
        out_specs=pl.BlockSpec((1, MOD_ROWS, tn), lambda l, n: (l, 0, n)),
        out_shape=jax.ShapeDtypeStruct((depth, MOD_ROWS, D3), F32),
        compiler_params=_cparams(("arbitrary", "arbitrary")),
        name="modulation",
    )(cc, mod_w, mod_b.reshape(depth, 1, D3))


def _proj_kernel(x_ref, sh_ref, sc_ref, w_ref, o_ref, h_ref):
    @pl.when(pl.program_id(1) == 0)
    def _():
        h_ref[...] = (x_ref[...] * (1.0 + sc_ref[0]) + sh_ref[0]).astype(BF16)

    o_ref[...] = _dot(h_ref[...], w_ref[...]).astype(o_ref.dtype)


def _proj_tn(n):
    return next(t for t in (1280, 1024, 768, 512, 256) if n % t == 0)


def _projection(xa, mod3, w, B, T, tn):
    R, D = xa.shape
    N = w.shape[1]
    n_lat = (B * T) // TM
    per_b = T // TM

    def mrow(m):
        return jnp.where(m < n_lat, m // per_b, B)

    return pl.pallas_call(
        _proj_kernel,
        grid=(R // TM, N // tn),
        in_specs=[pl.BlockSpec((TM, D), lambda m, n: (m, 0)),
                  pl.BlockSpec((1, 1, D), lambda m, n: (mrow(m), 0, 0)),
                  pl.BlockSpec((1, 1, D), lambda m, n: (mrow(m), 0, 1)),
                  pl.BlockSpec((D, tn), lambda m, n: (0, n))],
        out_specs=pl.BlockSpec((TM, tn), lambda m, n: (m, n)),
        out_shape=jax.ShapeDtypeStruct((R, N), BF16),
        scratch_shapes=[pltpu.VMEM((TM, D), BF16)],
        compiler_params=_cparams(("arbitrary", "arbitrary")),
        name="in_proj",
    )(xa, mod3, mod3, w)


def _wkv_kernel(rev, n_ctx_blk,
                r0_ref, k0_ref, v0_ref, wd_ref, ad_ref, vec_ref, lvec_ref, w2_ref, a2_ref,
                bd_ref, tri_ref, cm_ref,
                o_ref,
                h_ref, cr_ref, ck_ref, cv_ref, cw_ref, ca_ref,
                at_ref, rt_ref, bt_ref, kt_ref, bh_ref, kh_ref, vv_ref, ge_ref, bonus_ref):
    i = pl.program_id(1)
    C = CH
    RB, W = r0_ref.shape
    n_pairs = W // LANES
    n_ch = RB // C
    slot_w = slot_r = 0

    @pl.when(i == 0)
    def _():
        h_ref[...] = jnp.zeros_like(h_ref)
        for c_ref in (cr_ref, ck_ref, cv_ref, cw_ref, ca_ref):
            c_ref[...] = jnp.zeros_like(c_ref)

    seg_start = jnp.logical_or(i == 0, i == n_ctx_blk)
    row = lax.broadcasted_iota(jnp.int32, (RB, 1), 0)
    edge = row == (RB - 1 if rev else 0)

    def tshift(x_ref, c_ref, mu):
        xv = x_ref[...].astype(F32)
        nb = pltpu.roll(xv, (RB - 1) if rev else 1, axis=0)
        prev = jnp.where(seg_start, 0.0, c_ref[0:1, :])
        nb = jnp.where(edge, prev, nb)
        c_ref[0:1, :] = xv[0:1, :] if rev else xv[RB - 1:RB, :]
        return xv + (nb - xv) * mu

    def seg(xv):
        parts = [_dot(xv[:, g * 256:(g + 1) * 256].astype(BF16), bd_ref[...]) for g in range(W // 256)]
        return jnp.concatenate(parts, axis=1)

    vec = lambda j: vec_ref[j:j + 1, :]
    chunks = range(n_ch)
    rows = [slice(c * C, (c + 1) * C) for c in chunks]

    def prepare_shift():
        r = tshift(r0_ref, cr_ref, vec(0))
        k = tshift(k0_ref, ck_ref, vec(1))
        v = tshift(v0_ref, cv_ref, vec(2))
        wdl = jnp.tanh(tshift(wd_ref, cw_ref, lvec_ref[0:1, :]))
        w_pre = vec(3) + _dot(wdl.astype(BF16), w2_ref[...])
        adl = tshift(ad_ref, ca_ref, lvec_ref[1:2, :])
        a_pre = vec(4) + _dot(adl.astype(BF16), a2_ref[...])
        kk0 = k * vec(5)
        return r, k, v, w_pre, a_pre, kk0, seg(kk0 * kk0)

    def prepare_decay(r, k, v, w_pre, a_pre, kk0, kk_norm):
        a = jax.nn.sigmoid(a_pre)
        lw = -EXP_NEG_HALF * jax.nn.sigmoid(w_pre)
        kk = kk0 * lax.rsqrt(kk_norm + 1e-12)
        k2 = k * (1.0 + (a - 1.0) * vec(6))
        b_in = kk * a
        bonus_ref[slot_w] = seg(r * k2 * vec(7)) * v
        vv_ref[slot_w] = v.astype(BF16)
        lw_hi = lw.astype(BF16)
        lw_lo = (lw - lw_hi.astype(F32)).astype(BF16)
        for c in chunks:
            rs = rows[c]
            cum = _dot(tri_ref[...], lw_hi[rs]) + _dot(tri_ref[...], lw_lo[rs])
            e_in = jnp.exp(cum)
            e_out = jnp.exp(-cum)
            g_end = jnp.exp(cum[0:1, :] if rev else cum[C - 1:C, :])
            bt = b_in[rs] * e_out
            kt = k2[rs] * e_out
            at_ref[slot_w, rs, :] = (-kk[rs] * jnp.exp(cum - lw[rs])).astype(BF16)
            rt_ref[slot_w, rs, :] = (r[rs] * e_in).astype(BF16)
            bt_ref[slot_w, rs, :] = bt.astype(BF16)
            kt_ref[slot_w, rs, :] = kt.astype(BF16)
            bh_ref[slot_w, rs, :] = (bt * g_end).astype(BF16)
            kh_ref[slot_w, rs, :] = (kt * g_end).astype(BF16)
            ge_ref[slot_w, c:c + 1, :] = g_end

    prepare_decay(*prepare_shift())

    lane = lax.broadcasted_iota(jnp.int32, (1, LANES), 1)
    lo = lane < RWKV_HEAD
    lane2 = lax.broadcasted_iota(jnp.int32, (1, 2 * LANES), 1)
    lo2 = (lane2 % LANES) < RWKV_HEAD
    cm = cm_ref[...] > 0.5
    on_diag = (lax.broadcasted_iota(jnp.int32, (C, LANES), 0)
               == lax.broadcasted_iota(jnp.int32, (C, LANES), 1) % C)
    eye_c = on_diag.astype(F32)
    zeros_c = jnp.zeros((C, LANES), BF16)

    def first(xv, m=lo):
        return jnp.where(m, xv, jnp.zeros_like(xv))

    def second(xv, m=lo):
        return jnp.where(m, jnp.zeros_like(xv), xv)

    def bdiag(xv, m):
        xb = xv.astype(BF16)
        return jnp.concatenate([first(xb, m), second(xb, m)], axis=0)

    col = [slice(p * LANES, (p + 1) * LANES) for p in range(n_pairs)]

    def state_free(group):
        units = [(c, p) for c in group for p in range(n_pairs)]
        ld = lambda ref: {(c, p): ref[slot_r, rows[c], col[p]] for c, p in units}
        at_s, rt_s, bt_s, kt_s, bh_s, kh_s, v_s = (ld(ref) for ref in (at_ref, rt_ref, bt_ref, kt_ref,
                                                                      bh_ref, kh_ref, vv_ref))
        g_end_s = {(c, p): ge_ref[slot_r, c:c + 1, col[p]] for c, p in units}
        g0 = {u: jnp.where(cm, _dot_nt(jnp.concatenate([first(at_s[u]), first(rt_s[u])], axis=0),
                                       jnp.concatenate([bt_s[u], kt_s[u]], axis=0)), 0.0)
              for u in units}
        g1 = {u: jnp.where(cm, _dot_nt(jnp.concatenate([second(at_s[u]), second(rt_s[u])], axis=0),
                                       jnp.concatenate([kt_s[u], bt_s[u]], axis=0)), 0.0)
              for u in units}
        pcat = {u: jnp.where(lo, g0[u][:C], g1[u][:C]) for u in units}
        aksw = {u: jnp.where(lo, g1[u][:C], g0[u][:C]) for u in units}
        rbcat = {u: jnp.where(lo, g0[u][C:], g1[u][C:]) for u in units}
        rksw = {u: jnp.where(lo, g1[u][C:], g0[u][C:]) for u in units}
        akv = {u: _dot(aksw[u].astype(BF16), jnp.concatenate([second(v_s[u]), first(v_s[u])], axis=0))
               for u in units}

        tacc = {u: eye_c + pcat[u] for u in units}
        pw = {u: _dot(pcat[u].astype(BF16), bdiag(pcat[u], lo)) for u in units}
        n_sq = C.bit_length() - 1
        for j in range(1, n_sq):
            if j == n_sq - 1:
                tacc = {u: tacc[u] + _dot(pw[u].astype(BF16), bdiag(tacc[u], lo)) for u in units}
            else:
                both = {u: _dot(pw[u].astype(BF16), bdiag(jnp.concatenate([pw[u], tacc[u]], axis=1), lo2))
                        for u in units}
                pw = {u: both[u][:, :LANES] for u in units}
                tacc = {u: tacc[u] + both[u][:, LANES:] for u in units}
        av = {u: _dot(tacc[u].astype(BF16),
                      bdiag(jnp.concatenate([at_s[u], akv[u].astype(BF16)], axis=1), lo2))
              for u in units}

        rhs = {u: jnp.concatenate([
            bdiag(av[u], lo2),
            jnp.concatenate([zeros_c, second(v_s[u])], axis=1),
            jnp.concatenate([zeros_c, first(v_s[u])], axis=1)], axis=0) for u in units}
        out_c = {u: _dot(jnp.concatenate([rbcat[u], rksw[u]], axis=1).astype(BF16), rhs[u])
                 for u in units}
        out_d = {u: _dot_tn(jnp.concatenate([first(bh_s[u]), second(bh_s[u]), second(kh_s[u]),
                                             first(kh_s[u])], axis=0), rhs[u]) for u in units}
        md = {u: out_d[u][:C] + out_d[u][C:] for u in units}
        lhs_b = {u: jnp.concatenate([md[u][:, :LANES] + jnp.where(on_diag, g_end_s[u], 0.0),
                                     rt_s[u].astype(F32) + out_c[u][:, :LANES]], axis=0).astype(BF16)
                 for u in units}
        return lhs_b, {u: md[u][:, LANES:] for u in units}, out_c

    order = list(reversed(chunks)) if rev else list(chunks)
    lhs_b, n_cat, out_c = state_free(order)

    pairs = range(n_pairs)
    h = [h_ref[p] for p in pairs]
    y_rows = [None] * n_ch
    for c in order:
        ob = [_dot(lhs_b[c, p], bdiag(h[p], lo)) for p in pairs]
        h = [ob[p][:C] + n_cat[c, p] for p in pairs]
        y_rows[c] = jnp.concatenate([ob[p][C:] + out_c[c, p][:, LANES:] for p in pairs], axis=1)
    for p in pairs:
        h_ref[p] = h[p]
    y = jnp.concatenate(y_rows, axis=0)
    inv_n = 1.0 / RWKV_HEAD
    mu = seg(y) * inv_n
    d = y - mu
    var = seg(d * d) * inv_n
    o_ref[...] = d * lax.rsqrt(var + GN_EPS) * vec(8) + vec(9) + bonus_ref[slot_r]


def _wkv_direction(P, rev, B, T, L, RW, off, vecs, lvec, w2p, a2p):
    R = P.shape[0]
    C = CH
    RB = WKV_ROWS
    n_ctx, n_lat = L // RB, T // RB
    n_blk = n_ctx + n_lat
    lat0, ctx0 = 0, (B * T) // RB

    def block(b, i):
        if rev:
            return jnp.where(i < n_ctx, ctx0 + b * n_ctx + (n_ctx - 1 - i), lat0 + b * n_lat + (n_blk - 1 - i))
        return jnp.where(i < n_ctx, ctx0 + b * n_ctx + i, lat0 + b * n_lat + (i - n_ctx))

    t = jnp.arange(C)
    tri = (t[None, :] >= t[:, None]) if rev else (t[None, :] <= t[:, None])
    strict = (t[None, :] > t[:, None]) if rev else (t[None, :] < t[:, None])
    cm = jnp.concatenate([jnp.tile(strict, (1, 2)), jnp.tile(tri, (1, 2))], axis=0).astype(F32)
    g = jnp.arange(256) // RWKV_HEAD
    bd = (g[:, None] == g[None, :]).astype(BF16)
    n_pairs = RW // LANES
    col = lambda c0, w: (lambda b, i: (block(b, i), c0 // w))
    const = lambda b, i: (0, 0)
    slots = lambda rows, dt: pltpu.VMEM((1, rows, RW), dt)
    return pl.pallas_call(
        functools.partial(_wkv_kernel, rev, n_ctx),
        grid=(B, n_blk),
        in_specs=[pl.BlockSpec((RB, RW), col(off["r"], RW)),
                  pl.BlockSpec((RB, RW), col(off["k"], RW)),
                  pl.BlockSpec((RB, RW), col(off["v"], RW)),
                  pl.BlockSpec((RB, LANES), col(off["wd"], LANES)),
                  pl.BlockSpec((RB, LANES), col(off["ad"], LANES)),
                  pl.BlockSpec(vecs.shape, const),
                  pl.BlockSpec(lvec.shape, const),
                  pl.BlockSpec(w2p.shape, const),
                  pl.BlockSpec(a2p.shape, const),
                  pl.BlockSpec(bd.shape, const),
                  pl.BlockSpec((C, C), const),
                  pl.BlockSpec(cm.shape, const)],
        out_specs=pl.BlockSpec((RB, RW), lambda b, i: (block(b, i), 0)),
        out_shape=jax.ShapeDtypeStruct((R, RW), F32),
        scratch_shapes=[pltpu.VMEM((n_pairs, C, LANES), F32),
                        pltpu.VMEM((8, RW), F32), pltpu.VMEM((8, RW), F32), pltpu.VMEM((8, RW), F32),
                        pltpu.VMEM((8, LANES), F32), pltpu.VMEM((8, LANES), F32),
                        slots(RB, BF16), slots(RB, BF16), slots(RB, BF16), slots(RB, BF16),
                        slots(RB, BF16), slots(RB, BF16), slots(RB, BF16),
                        slots(8, F32), slots(RB, F32)],
        compiler_params=_cparams(("arbitrary", "arbitrary")),
        name="wkv_rev" if rev else "wkv_fwd",
    )(P, P, P, P, P, vecs, lvec, w2p, a2p, bd, tri.astype(BF16), cm)


def _residual_ln(x, y, gate, g, b, alpha):
    zz = alpha * x + gate * y
    mu = jnp.mean(zz, axis=-1, keepdims=True)
    d = zz - mu
    var = jnp.mean(d * d, axis=-1, keepdims=True)
    return d * lax.rsqrt(var + LN_EPS) * g + b


def _silu(g):
    return g * jax.nn.sigmoid(g)


def _sub_tiles():
    return [slice(j * BLK, (j + 1) * BLK) for j in range(ROWS2 // BLK)]


def _pool_mix(sub, pos, seg_len, u_cur, u_prev, u_next, pw_ref):
    G = u_cur.shape[-1] // len(POOL_WINDOWS)
    t = pos + lax.broadcasted_iota(jnp.int32, (BLK, 1), 0)
    s_cur = pos + lax.broadcasted_iota(jnp.int32, (BLK, BLK), 1)
    s_prev = pos - HALO + lax.broadcasted_iota(jnp.int32, (BLK, HALO), 1)
    s_next = pos + BLK + lax.broadcasted_iota(jnp.int32, (BLK, HALO), 1)
    mixed = []
    for gi, win in enumerate(POOL_WINDOWS):
        lo = jnp.clip(t - win // 2, 0, seg_len)
        hi = jnp.clip(t - win // 2 + win, 0, seg_len)
        inv = 1.0 / (hi - lo).astype(F32)

        def band(sidx):
            return jnp.where(sidx >= lo, jnp.where(sidx < hi, 1.0, 0.0), 0.0).astype(BF16)

        cs = slice(gi * G, (gi + 1) * G)
        sums = (_dot(band(s_cur), u_cur[:, cs]) + _dot(band(s_prev), u_prev[:, cs])
                + _dot(band(s_next), u_next[:, cs]))
        pooled = sums * inv - u_cur[:, cs].astype(F32)
        mixed.append(_dot(pooled.astype(BF16), pw_ref[gi]))
    return jnp.concatenate(mixed, axis=1)


def _even_out_kernel(alpha, n_lat_blk, T, L,
                     x_ref, ac_ref, ap_ref, an_ref, gate_ref, yf_ref, yr_ref, mg_ref,
                     pw_ref, ps_ref, wo_ref, lg_ref, lb_ref, o_ref):
    m = pl.program_id(0)
    subs = _sub_tiles()
    acts = []
    for j, sub in enumerate(subs):
        m2 = m * len(subs) + j
        is_lat = m2 < n_lat_blk
        seg_len = jnp.where(is_lat, T, L)
        pos = jnp.where(is_lat, (m2 % (T // BLK)) * BLK, ((m2 - n_lat_blk) % (L // BLK)) * BLK)
        u_prev = ap_ref[...] if j == 0 else ac_ref[j * BLK - HALO:j * BLK, :]
        u_next = an_ref[...] if j == len(subs) - 1 else ac_ref[(j + 1) * BLK:(j + 1) * BLK + HALO, :]
        a_pool = _pool_mix(sub, pos, seg_len, ac_ref[sub, :], u_prev, u_next, pw_ref) * ps_ref[...]
        act = (jnp.concatenate([a_pool, yf_ref[sub, :] + yr_ref[sub, :]], axis=1)
               * _silu(gate_ref[sub, :].astype(F32)))
        acts.append(act.astype(BF16))
    ys = [_dot(act, wo_ref[...]) for act in acts]
    for sub, y in zip(subs, ys):
        o_ref[sub, :] = _residual_ln(x_ref[sub, :], y, mg_ref[0], lg_ref[...], lb_ref[...], alpha)


def _odd_out_kernel(alpha, two_att, n_lat2, x_ref, *refs):
    if two_att:
        att_ref, attc_ref, gate_ref, mg_ref, wo_ref, lg_ref, lb_ref, o_ref = refs
        is_lat = pl.program_id(0) < n_lat2
    else:
        att_ref, gate_ref, mg_ref, wo_ref, lg_ref, lb_ref, o_ref = refs
    subs = _sub_tiles()
    acts = []
    for sub in subs:
        att = att_ref[sub, :]
        if two_att:
            att = jnp.where(is_lat, att, attc_ref[sub, :])
        acts.append((att.astype(F32) * _silu(gate_ref[sub, :].astype(F32))).astype(BF16))
    ys = [_dot(act, wo_ref[...]) for act in acts]
    for sub, y in zip(subs, ys):
        o_ref[sub, :] = _residual_ln(x_ref[sub, :], y, mg_ref[0], lg_ref[...], lb_ref[...], alpha)


def _mod_row(m, n_lat2, per_b, B):
    return jnp.where(m < n_lat2, m // per_b, B)


def _even_out(xa, P, yf, yr, mod3, pool_w, pool_scale, w_out, ln_g, ln_b, alpha, B, T, L, n_rows, off):
    R, D = xa.shape
    PW = pool_scale.shape[-1]
    n_lat2 = (B * T) // ROWS2
    per_b = T // ROWS2
    hb = ROWS2 // HALO
    last_halo = R // HALO - 1
    const2 = lambda m: (0, 0)
    return pl.pallas_call(
        functools.partial(_even_out_kernel, alpha, (B * T) // BLK, T, L),
        grid=(n_rows // ROWS2,),
        in_specs=[pl.BlockSpec((ROWS2, D), lambda m: (m, 0)),
                  pl.BlockSpec((ROWS2, PW), lambda m: (m, off["a"] // PW)),
                  pl.BlockSpec((HALO, PW), lambda m: (jnp.maximum(m * hb - 1, 0), off["a"] // PW)),
                  pl.BlockSpec((HALO, PW), lambda m: (jnp.minimum((m + 1) * hb, last_halo), off["a"] // PW)),
                  pl.BlockSpec((ROWS2, D), lambda m: (m, off["gate"] // D)),
                  pl.BlockSpec((ROWS2, D - PW), lambda m: (m, 0)),
                  pl.BlockSpec((ROWS2, D - PW), lambda m: (m, 0)),
                  pl.BlockSpec((1, 1, D), lambda m: (_mod_row(m, n_lat2, per_b, B), 0, 2)),
                  pl.BlockSpec(pool_w.shape, lambda m: (0, 0, 0)),
                  pl.BlockSpec((1, PW), const2),
                  pl.BlockSpec(w_out.shape, const2),
                  pl.BlockSpec((1, D), const2),
                  pl.BlockSpec((1, D), const2)],
        out_specs=pl.BlockSpec((ROWS2, D), lambda m: (m, 0)),
        out_shape=jax.ShapeDtypeStruct((n_rows, D), F32),
        compiler_params=_cparams(("arbitrary",)),
        name="even_out",
    )(xa, P, P, P, P, yf, yr, mod3, pool_w, pool_scale, w_out, ln_g, ln_b)


def _odd_out(xa, P, att, att_ctx, mod3, w_out, ln_g, ln_b, alpha, B, T, n_rows, off):
    R, D = xa.shape
    n_lat2 = (B * T) // ROWS2
    per_b = T // ROWS2
    const2 = lambda m: (0, 0)
    two_att = att_ctx is not None
    att_specs = [pl.BlockSpec((ROWS2, D), lambda m: (jnp.minimum(m, n_lat2 - 1), 0))]
    att_args = [att]
    if two_att:
        att_specs.append(pl.BlockSpec((ROWS2, D), lambda m: (jnp.maximum(m - n_lat2, 0), 0)))
        att_args.append(att_ctx)
    return pl.pallas_call(
        functools.partial(_odd_out_kernel, alpha, two_att, n_lat2),
        grid=(n_rows // ROWS2,),
        in_specs=[pl.BlockSpec((ROWS2, D), lambda m: (m, 0))] + att_specs + [
                  pl.BlockSpec((ROWS2, D), lambda m: (m, off["gate"] // D)),
                  pl.BlockSpec((1, 1, D), lambda m: (_mod_row(m, n_lat2, per_b, B), 0, 2)),
                  pl.BlockSpec(w_out.shape, const2),
                  pl.BlockSpec((1, D), const2),
                  pl.BlockSpec((1, D), const2)],
        out_specs=pl.BlockSpec((ROWS2, D), lambda m: (m, 0)),
        out_shape=jax.ShapeDtypeStruct((n_rows, D), F32),
        compiler_params=_cparams(("arbitrary",)),
        name="odd_out",
    )(xa, *att_args, P, mod3, w_out, ln_g, ln_b)


def _rope(xv, cos, sin):
    q = HEAD_DIM // 4
    lane = lax.broadcasted_iota(jnp.int32, (1, HEAD_DIM), 1)
    first_half = (lane % (2 * q)) < q
    partner = jnp.where(first_half, pltpu.roll(xv, HEAD_DIM - q, axis=1), pltpu.roll(xv, q, axis=1))
    return xv * cos + partner * sin


def _rms(xv, g):
    return xv * lax.rsqrt(jnp.mean(xv * xv, axis=-1, keepdims=True) + LN_EPS) * g


_Q_SCALE = HEAD_DIM ** -0.5 * LOG2_E


def _softmax_pv(s, v_ones):
    mx = jnp.max(s, axis=-1, keepdims=True)
    pr = jnp.exp2((s - mx).astype(BF16))
    o = _dot(pr, v_ones)
    return o[:, :HEAD_DIM] / o[:, HEAD_DIM:]


def _attn_kernel(q_ref, kc_ref, kl_ref, vc_ref, vl_ref, cq_ref, sq_ref, cos_ref, sin_ref, qg_ref, kg_ref,
                 o_ref, ks_ref, vs_ref):
    @pl.when(pl.program_id(2) == 0)
    def _():
        kf = jnp.concatenate([kc_ref[...], kl_ref[...]], axis=0).astype(F32)
        ks_ref[...] = _rope(_rms(kf, kg_ref[...]), cos_ref[...], sin_ref[...]).astype(BF16)
        vv = jnp.concatenate([vc_ref[...], vl_ref[...]], axis=0)
        vs_ref[...] = jnp.concatenate([vv, jnp.ones_like(vv)], axis=1)

    units = [(sub, h) for sub in _sub_tiles() for h in range(GROUP)]

    def scores(sub, h):
        qh = q_ref[sub, h * HEAD_DIM:(h + 1) * HEAD_DIM].astype(F32)
        qh = _rope(_rms(qh, qg_ref[...]), cq_ref[sub, :], sq_ref[sub, :]) * _Q_SCALE
        return _dot_nt(qh.astype(BF16), ks_ref[...])

    s_next = scores(*units[0])
    for n, (sub, h) in enumerate(units):
        s_cur = s_next
        if n + 1 < len(units):
            s_next = scores(*units[n + 1])
        o_ref[sub, h * HEAD_DIM:(h + 1) * HEAD_DIM] = _softmax_pv(s_cur, vs_ref[...]).astype(o_ref.dtype)


def _attn_ctx_kernel(q_ref, kc_ref, vc_ref, qg_ref, kg_ref, o_ref):
    kn = _rms(kc_ref[...].astype(F32), kg_ref[...]).astype(BF16)
    vv = vc_ref[...]
    v_ones = jnp.concatenate([vv, jnp.ones_like(vv)], axis=1)
    for h in range(GROUP):
        hs = slice(h * HEAD_DIM, (h + 1) * HEAD_DIM)
        qh = _rms(q_ref[:, hs].astype(F32), qg_ref[...]) * _Q_SCALE
        o_ref[:, hs] = _softmax_pv(_dot_nt(qh.astype(BF16), kn), v_ones).astype(o_ref.dtype)


def _attention(P, cos_t, sin_t, qg, kg, B, T, L, D, off):
    KVH = D // HEAD_DIM // GROUP
    QW = GROUP * HEAD_DIM
    per_b = T // ROWS2
    kcol = off["k"] // HEAD_DIM
    vcol = off["v"] // HEAD_DIM
    const = lambda b, g, i: (0, 0)
    return pl.pallas_call(
        _attn_kernel,
        grid=(B, KVH, per_b),
        in_specs=[pl.BlockSpec((ROWS2, QW), lambda b, g, i: (b * per_b + i, g)),
                  pl.BlockSpec((L, HEAD_DIM), lambda b, g, i: ((B * T) // L + b, kcol + g)),
                  pl.BlockSpec((T, HEAD_DIM), lambda b, g, i: (b, kcol + g)),
                  pl.BlockSpec((L, HEAD_DIM), lambda b, g, i: ((B * T) // L + b, vcol + g)),
                  pl.BlockSpec((T, HEAD_DIM), lambda b, g, i: (b, vcol + g)),
                  pl.BlockSpec((ROWS2, HEAD_DIM), lambda b, g, i: (i, 0)),
                  pl.BlockSpec((ROWS2, HEAD_DIM), lambda b, g, i: (i, 0)),
                  pl.BlockSpec((L + T, HEAD_DIM), const),
                  pl.BlockSpec((L + T, HEAD_DIM), const),
                  pl.BlockSpec((1, HEAD_DIM), const),
                  pl.BlockSpec((1, HEAD_DIM), const)],
        out_specs=pl.BlockSpec((ROWS2, QW), lambda b, g, i: (b * per_b + i, g)),
        out_shape=jax.ShapeDtypeStruct((B * T, D), BF16),
        scratch_shapes=[pltpu.VMEM((L + T, HEAD_DIM), BF16), pltpu.VMEM((L + T, 2 * HEAD_DIM), BF16)],
        compiler_params=_cparams(("arbitrary", "arbitrary", "arbitrary")),
        name="attention",
    )(P, P, P, P, P, cos_t[L:], sin_t[L:], cos_t, sin_t, qg, kg)


def _attention_ctx(P, qg, kg, B, T, L, D, off):
    KVH = D // HEAD_DIM // GROUP
    QW = GROUP * HEAD_DIM
    ctx0 = (B * T) // L
    kcol = off["k"] // HEAD_DIM
    vcol = off["v"] // HEAD_DIM
    const = lambda b, g: (0, 0)
    return pl.pallas_call(
        _attn_ctx_kernel,
        grid=(B, KVH),
        in_specs=[pl.BlockSpec((L, QW), lambda b, g: (ctx0 + b, g)),
                  pl.BlockSpec((L, HEAD_DIM), lambda b, g: (ctx0 + b, kcol + g)),
                  pl.BlockSpec((L, HEAD_DIM), lambda b, g: (ctx0 + b, vcol + g)),
                  pl.BlockSpec((1, HEAD_DIM), const),
                  pl.BlockSpec((1, HEAD_DIM), const)],
        out_specs=pl.BlockSpec((L, QW), lambda b, g: (b, g)),
        out_shape=jax.ShapeDtypeStruct((B * L, D), BF16),
        compiler_params=_cparams(("arbitrary", "arbitrary")),
        name="attention_ctx",
    )(P, P, P, qg, kg)


def _rope_tables(T, L):
    q = HEAD_DIM // 4
    rows = T // GRID_W
    row = jnp.repeat(jnp.arange(rows, dtype=F32), GRID_W)
    col = jnp.tile(jnp.arange(GRID_W, dtype=F32), rows)
    inv_freq = ROPE_THETA ** (-jnp.arange(0, 2 * q, 2, dtype=F32) / (2 * q))
    ar, ac = row[:, None] * inv_freq, col[:, None] * inv_freq
    cos = jnp.concatenate([jnp.cos(ar), jnp.cos(ar), jnp.cos(ac), jnp.cos(ac)], axis=1)
    sin = jnp.concatenate([-jnp.sin(ar), jnp.sin(ar), -jnp.sin(ac), jnp.sin(ac)], axis=1)
    cos = jnp.concatenate([jnp.ones((L, HEAD_DIM), F32), cos], axis=0)
    sin = jnp.concatenate([jnp.zeros((L, HEAD_DIM), F32), sin], axis=0)
    return cos, sin


def _even_layer(xa, mod3, last, B, T, L, alpha, w_in, w_out, pool_w, pool_scale, mu_rkv, mu_lora, w0, w2,
                a0, a2, k_k, k_a, r_k, gn_g, gn_b, ln_g, ln_b):
    R, D = xa.shape
    PW = pool_scale.shape[-1]
    RW = k_k.shape[-1]
    n_split = PW + 3 * RW
    lora2 = 2 * LORA_W
    w = jnp.concatenate([w_in[:, :n_split], w_in[:, n_split + 2 * lora2:],
                         w_in[:, n_split:n_split + 2 * lora2]], axis=1).astype(BF16)
    off = {"a": 0, "r": PW, "k": PW + RW, "v": PW + 2 * RW, "gate": n_split,
           "wd": n_split + PW + RW, "ad": n_split + PW + RW + lora2}
    P = _projection(xa, mod3, w, B, T, _proj_tn(w.shape[1]))

    ys = []
    for d, rev in enumerate((False, True)):
        vecs = jnp.zeros((16, RW), F32)
        rows = [mu_rkv[d, 0], mu_rkv[d, 1], mu_rkv[d, 2], w0[d], a0[d], k_k, k_a, r_k.reshape(RW),
                gn_g[d], gn_b[d]]
        vecs = vecs.at[:len(rows)].set(jnp.stack(rows))
        lvec = jnp.zeros((8, lora2), F32).at[0].set(mu_lora[:, 0, :].reshape(lora2)).at[1].set(
            mu_lora[:, 1, :].reshape(lora2))
        w2p = jnp.zeros((lora2, RW), F32).at[d * LORA_W:(d + 1) * LORA_W].set(w2[d]).astype(BF16)
        a2p = jnp.zeros((lora2, RW), F32).at[d * LORA_W:(d + 1) * LORA_W].set(a2[d]).astype(BF16)
        ys.append(_wkv_direction(P, rev, B, T, L, RW, off, vecs, lvec, w2p, a2p))

    n_rows = B * T if last else R
    return _even_out(xa, P, ys[0], ys[1], mod3, pool_w.astype(BF16), pool_scale.reshape(1, PW),
                     w_out.astype(BF16), ln_g.reshape(1, D), ln_b.reshape(1, D), alpha, B, T, L, n_rows, off)


def _odd_layer(xa, mod3, last, B, T, L, alpha, w_in, w_out, q_g, k_g, tables, ln_g, ln_b):
    R, D = xa.shape
    KVW = D // GROUP
    w = jnp.concatenate([w_in[:, :D], w_in[:, D + 2 * KVW:], w_in[:, D:D + 2 * KVW]], axis=1).astype(BF16)
    off = {"q": 0, "gate": D, "k": 2 * D, "v": 2 * D + KVW}
    P = _projection(xa, mod3, w, B, T, _proj_tn(w.shape[1]))
    qg, kg = q_g.reshape(1, HEAD_DIM), k_g.reshape(1, HEAD_DIM)
    att = _attention(P, tables[0], tables[1], qg, kg, B, T, L, D, off)
    att_ctx = None if last else _attention_ctx(P, qg, kg, B, T, L, D, off)
    n_rows = B * T if last else R
    return _odd_out(xa, P, att, att_ctx, mod3, w_out.astype(BF16), ln_g.reshape(1, D), ln_b.reshape(1, D),
                    alpha, B, T, n_rows, off)


def kernel(x, c, ctx, c_ctx, mod_w, mod_b, ln_g, ln_b, ev_w_in, ev_w_out, pool_w, pool_scale, rw_mu_rkv,
           rw_mu_lora, rw_w0, rw_w2, rw_a0, rw_a2, rw_k_k, rw_k_a, rw_r_k, rw_gn_g, rw_gn_b, od_w_in,
           od_w_out, q_norm_g, k_norm_g):
    B, T, D = x.shape
    L = ctx.shape[1]
    depth = mod_w.shape[0]
    assert L == BLK and T % TM == 0 and (B * L) % TM == 0 and T % GRID_W == 0 and B < MOD_ROWS
    assert D // HEAD_DIM % GROUP == 0 and (D // 2) % 256 == 0 and L % WKV_ROWS == 0 and 2 * CH == LANES
    alpha = (2 * depth) ** 0.25
    xa = jnp.concatenate([x.reshape(B * T, D), ctx.reshape(B * L, D)], axis=0)
    mod = _modulation(c, c_ctx, mod_w, mod_b)
    tables = _rope_tables(T, L)
    for l in range(depth):
        last = l == depth - 1
        i = l // 2
        mod3 = mod[l].reshape(MOD_ROWS, 1, 3 * D)
        if l % 2 == 0:
            xa = _even_layer(xa, mod3, last, B, T, L, alpha, ev_w_in[i], ev_w_out[i], pool_w[i], pool_scale[i],
                             rw_mu_rkv[i], rw_mu_lora[i], rw_w0[i], rw_w2[i], rw_a0[i], rw_a2[i], rw_k_k[i],
                             rw_k_a[i], rw_r_k[i], rw_gn_g[i], rw_gn_b[i], ln_g[l], ln_b[l])
        else:
            xa = _odd_layer(xa, mod3, last, B, T, L, alpha, od_w_in[i], od_w_out[i], q_norm_g[i], k_norm_g[i],
                            tables, ln_g[l], ln_b[l])
    return xa[:B * T].reshape(B, T, D)
```

```python
import functools

import jax
import jax.numpy as jnp
from jax import lax
from jax.experimental import pallas as pl
from jax.experimental.pallas import tpu as pltpu

F32 = jnp.float32
BF16 = jnp.bfloat16

LANES = 128
VMEM_LIMIT = 52 * 1024 * 1024

GRID_W = 64
POOL_WINDOWS = (2, 4, 8, 16)
RWKV_HEAD = 64
LORA_W = 64
GN_EPS = 64e-5
HEAD_DIM = 128
GROUP = 4
ROPE_THETA = 10000.0
LN_EPS = 1e-6
LOG2_E = 1.4426950408889634
EXP_NEG_HALF = 0.6065306597126334

MOD_ROWS = 24
TM = 1024
BLK = 256
ROWS2 = 2 * BLK
OUT_SUB = 256
CH = 64
WKV_ROWS = 256
HALO = 16


def _dot(a, b):
    return jnp.dot(a, b, preferred_element_type=F32)


def _dot_nt(a, b):
    return lax.dot_general(a, b, (((1,), (1,)), ((), ())), preferred_element_type=F32)


def _dot_tn(a, b):
    return lax.dot_general(a, b, (((0,), (0,)), ((), ())), preferred_element_type=F32)


def _cparams(sem):
    return pltpu.CompilerParams(dimension_semantics=sem, vmem_limit_bytes=VMEM_LIMIT)


def _mod_kernel(c_ref, w_ref, b_ref, o_ref):
    cc = c_ref[...]
    s = cc * jax.nn.sigmoid(cc)
    o_ref[0] = _dot(s.astype(BF16), w_ref[0].astype(BF16)) + b_ref[0]


def _modulation(c, c_ctx, mod_w, mod_b):
    depth, D, D3 = mod_w.shape
    B = c.shape[0]
    cc = jnp.zeros((MOD_ROWS, D), F32).at[:B].set(c).at[B].set(c_ctx)
    tn = 1024
    return pl.pallas_call(
        _mod_kernel,
        grid=(depth, D3 // tn),
        in_specs=[pl.BlockSpec((MOD_ROWS, D), lambda l, n: (0, 0)),
                  pl.BlockSpec((1, D, tn), lambda l, n: (l, 0, n)),
                  pl.BlockSpec((1, 1, tn), lambda l, n: (l, 0, n))],
        out_specs=pl.BlockSpec((1, MOD_ROWS, tn), lambda l, n: (l, 0, n)),
        out_shape=jax.ShapeDtypeStruct((depth, MOD_ROWS, D3), F32),
        compiler_params=_cparams(("arbitrary", "arbitrary")),
        name="modulation",
    )(cc, mod_w, mod_b.reshape(depth, 1, D3))


def _proj_kernel(x_ref, sh_ref, sc_ref, w_ref, o_ref, h_ref):
    @pl.when(pl.program_id(1) == 0)
    def _():
        h_ref[...] = (x_ref[...] * (1.0 + sc_ref[0]) + sh_ref[0]).astype(BF16)

    o_ref[...] = _dot(h_ref[...], w_ref[...]).astype(o_ref.dtype)


def _proj_tn(n):
    return next(t for t in (1280, 1024, 768, 512, 256) if n % t == 0)


def _projection(xa, mod3, w, B, T, tn):
    R, D = xa.shape
    N = w.shape[1]
    n_lat = (B * T) // TM
    per_b = T // TM

    def mrow(m):
        return jnp.where(m < n_lat, m // per_b, B)

    return pl.pallas_call(
        _proj_kernel,
        grid=(R // TM, N // tn),
        in_specs=[pl.BlockSpec((TM, D), lambda m, n: (m, 0)),
                  pl.BlockSpec((1, 1, D), lambda m, n: (mrow(m), 0, 0)),
                  pl.BlockSpec((1, 1, D), lambda m, n: (mrow(m), 0, 1)),
                  pl.BlockSpec((D, tn), lambda m, n: (0, n))],
        out_specs=pl.BlockSpec((TM, tn), lambda m, n: (m, n)),
        out_shape=jax.ShapeDtypeStruct((R, N), BF16),
        scratch_shapes=[pltpu.VMEM((TM, D), BF16)],
        compiler_params=_cparams(("arbitrary", "arbitrary")),
        name="in_proj",
    )(xa, mod3, mod3, w)


def _wkv_kernel(rev, n_ctx_blk,
                r0_ref, k0_ref, v0_ref, wd_ref, ad_ref, vec_ref, lvec_ref, w2_ref, a2_ref,
                bd_ref, tri_ref, cm_ref,
                o_ref,
                h_ref, cr_ref, ck_ref, cv_ref, cw_ref, ca_ref,
                at_ref, rt_ref, bt_ref, kt_ref, bh_ref, kh_ref, vv_ref, ge_ref, bonus_ref):
    i = pl.program_id(1)
    C = CH
    RB, W = r0_ref.shape
    n_pairs = W // LANES
    n_ch = RB // C
    slot_w = slot_r = 0

    @pl.when(i == 0)
    def _():
        h_ref[...] = jnp.zeros_like(h_ref)
        for c_ref in (cr_ref, ck_ref, cv_ref, cw_ref, ca_ref):
            c_ref[...] = jnp.zeros_like(c_ref)

    seg_start = jnp.logical_or(i == 0, i == n_ctx_blk)
    SUB = 8
    edge = lax.broadcasted_iota(jnp.int32, (SUB, 1), 0) == (SUB - 1 if rev else 0)

    def tshift(x_ref, c_ref, mu):
        xv = x_ref[...].astype(F32)
        nb = pltpu.roll(xv, (RB - 1) if rev else 1, axis=0)
        prev = jnp.where(seg_start, 0.0, c_ref[0:1, :])
        if rev:
            nb = jnp.concatenate([nb[:RB - SUB], jnp.where(edge, prev, nb[RB - SUB:])], axis=0)
        else:
            nb = jnp.concatenate([jnp.where(edge, prev, nb[:SUB]), nb[SUB:]], axis=0)
        c_ref[0:1, :] = xv[0:1, :] if rev else xv[RB - 1:RB, :]
        return xv + (nb - xv) * mu

    def seg(xv):
        parts = [_dot(xv[:, g * 256:(g + 1) * 256].astype(BF16), bd_ref[...]) for g in range(W // 256)]
        return jnp.concatenate(parts, axis=1)

    vec = lambda j: vec_ref[j:j + 1, :]
    chunks = range(n_ch)
    rows = [slice(c * C, (c + 1) * C) for c in chunks]

    def prepare_shift():
        r = tshift(r0_ref, cr_ref, vec(0))
        k = tshift(k0_ref, ck_ref, vec(1))
        v = tshift(v0_ref, cv_ref, vec(2))
        wdl = jnp.tanh(tshift(wd_ref, cw_ref, lvec_ref[0:1, :]))
        w_pre = vec(3) + _dot(wdl.astype(BF16), w2_ref[...])
        adl = tshift(ad_ref, ca_ref, lvec_ref[1:2, :])
        a_pre = vec(4) + _dot(adl.astype(BF16), a2_ref[...])
        kk0 = k * vec(5)
        return r, k, v, w_pre, a_pre, kk0, seg(kk0 * kk0)

    def prepare_decay(r, k, v, w_pre, a_pre, kk0, kk_norm):
        a = jax.nn.sigmoid(a_pre)
        lw = -EXP_NEG_HALF * jax.nn.sigmoid(w_pre)
        kk = kk0 * lax.rsqrt(kk_norm + 1e-12)
        k2 = k * (1.0 + (a - 1.0) * vec(6))
        b_in = kk * a
        bonus_ref[slot_w] = seg(r * k2 * vec(7)) * v
        vv_ref[slot_w] = v.astype(BF16)
        lw_hi = lw.astype(BF16)
        lw_lo = (lw - lw_hi.astype(F32)).astype(BF16)
        for c in chunks:
            rs = rows[c]
            cum = _dot(tri_ref[...], lw_hi[rs]) + _dot(tri_ref[...], lw_lo[rs])
            e_in = jnp.exp(cum)
            e_out = jnp.exp(-cum)
            g_end = jnp.exp(cum[0:1, :] if rev else cum[C - 1:C, :])
            bt = b_in[rs] * e_out
            kt = k2[rs] * e_out
            at_ref[slot_w, rs, :] = (-kk[rs] * jnp.exp(cum - lw[rs])).astype(BF16)
            rt_ref[slot_w, rs, :] = (r[rs] * e_in).astype(BF16)
            bt_ref[slot_w, rs, :] = bt.astype(BF16)
            kt_ref[slot_w, rs, :] = kt.astype(BF16)
            bh_ref[slot_w, rs, :] = (bt * g_end).astype(BF16)
            kh_ref[slot_w, rs, :] = (kt * g_end).astype(BF16)
            ge_ref[slot_w, c:c + 1, :] = g_end

    prepare_decay(*prepare_shift())

    lane = lax.broadcasted_iota(jnp.int32, (1, LANES), 1)
    lo = lane < RWKV_HEAD
    lane2 = lax.broadcasted_iota(jnp.int32, (1, 2 * LANES), 1)
    lo2 = (lane2 % LANES) < RWKV_HEAD
    cm = cm_ref[...] > 0.5
    t_idx = lax.broadcasted_iota(jnp.int32, (C, LANES), 0)
    j_idx = lax.broadcasted_iota(jnp.int32, (C, LANES), 1) % C
    on_diag = t_idx == j_idx
    eye_c = on_diag.astype(F32)
    same_block = [(t_idx >> l) == (j_idx >> l) for l in range(C.bit_length())]
    off_block = [jnp.logical_and(same_block[l + 1], jnp.logical_not(same_block[l]))
                 for l in range(C.bit_length() - 1)]
    zeros_c = jnp.zeros((C, LANES), BF16)

    def first(xv, m=lo):
        return jnp.where(m, xv, jnp.zeros_like(xv))

    def second(xv, m=lo):
        return jnp.where(m, jnp.zeros_like(xv), xv)

    def bdiag(xv, m):
        xb = xv.astype(BF16)
        return jnp.concatenate([first(xb, m), second(xb, m)], axis=0)

    col = [slice(p * LANES, (p + 1) * LANES) for p in range(n_pairs)]

    def state_free(group):
        units = [(c, p) for c in group for p in range(n_pairs)]
        ld = lambda ref: {(c, p): ref[slot_r, rows[c], col[p]] for c, p in units}
        at_s, rt_s, bt_s, kt_s, bh_s, kh_s, v_s = (ld(ref) for ref in (at_ref, rt_ref, bt_ref, kt_ref,
                                                                      bh_ref, kh_ref, vv_ref))
        g_end_s = {(c, p): ge_ref[slot_r, c:c + 1, col[p]] for c, p in units}
        g0 = {u: jnp.where(cm, _dot_nt(jnp.concatenate([first(at_s[u]), first(rt_s[u])], axis=0),
                                       jnp.concatenate([bt_s[u], kt_s[u]], axis=0)), 0.0)
              for u in units}
        g1 = {u: jnp.where(cm, _dot_nt(jnp.concatenate([second(at_s[u]), second(rt_s[u])], axis=0),
                                       jnp.concatenate([kt_s[u], bt_s[u]], axis=0)), 0.0)
              for u in units}
        pcat = {u: jnp.where(lo, g0[u][:C], g1[u][:C]) for u in units}
        aksw = {u: jnp.where(lo, g1[u][:C], g0[u][:C]) for u in units}
        rbcat = {u: jnp.where(lo, g0[u][C:], g1[u][C:]) for u in units}
        rksw = {u: jnp.where(lo, g1[u][C:], g0[u][C:]) for u in units}
        akv = {u: _dot(aksw[u].astype(BF16), jnp.concatenate([second(v_s[u]), first(v_s[u])], axis=0))
               for u in units}

        tacc = {u: eye_c + jnp.where(same_block[1], pcat[u], 0.0) for u in units}
        for lvl in range(1, C.bit_length() - 1):
            off = {u: jnp.where(off_block[lvl], pcat[u], 0.0) for u in units}
            xo = {u: _dot(off[u].astype(BF16), bdiag(tacc[u], lo)) for u in units}
            tacc = {u: tacc[u] + _dot(tacc[u].astype(BF16), bdiag(xo[u], lo)) for u in units}
        av = {u: _dot(tacc[u].astype(BF16),
                      bdiag(jnp.concatenate([at_s[u], akv[u].astype(BF16)], axis=1), lo2))
              for u in units}

        rhs = {u: jnp.concatenate([
            bdiag(av[u], lo2),
            jnp.concatenate([zeros_c, second(v_s[u])], axis=1),
            jnp.concatenate([zeros_c, first(v_s[u])], axis=1)], axis=0) for u in units}
        out_c = {u: _dot(jnp.concatenate([rbcat[u], rksw[u]], axis=1).astype(BF16), rhs[u])
                 for u in units}
        out_d = {u: _dot_tn(jnp.concatenate([first(bh_s[u]), second(bh_s[u]), second(kh_s[u]),
                                             first(kh_s[u])], axis=0), rhs[u]) for u in units}
        md = {u: out_d[u][:C] + out_d[u][C:] for u in units}
        lhs_b = {u: jnp.concatenate([md[u][:, :LANES] + jnp.where(on_diag, g_end_s[u], 0.0),
                                     rt_s[u].astype(F32) + out_c[u][:, :LANES]], axis=0).astype(BF16)
                 for u in units}
        return lhs_b, {u: md[u][:, LANES:] for u in units}, out_c

    order = list(reversed(chunks)) if rev else list(chunks)
    lhs_b, n_cat, out_c = state_free(order)

    pairs = range(n_pairs)
    h = [h_ref[p] for p in pairs]
    y_rows = [None] * n_ch
    for c in order:
        ob = [_dot(lhs_b[c, p], bdiag(h[p], lo)) for p in pairs]
        h = [ob[p][:C] + n_cat[c, p] for p in pairs]
        y_rows[c] = jnp.concatenate([ob[p][C:] + out_c[c, p][:, LANES:] for p in pairs], axis=1)
    for p in pairs:
        h_ref[p] = h[p]
    y = jnp.concatenate(y_rows, axis=0)
    inv_n = 1.0 / RWKV_HEAD
    mu = seg(y) * inv_n
    d = y - mu
    var = seg(d * d) * inv_n
    o_ref[...] = d * lax.rsqrt(var + GN_EPS) * vec(8) + vec(9) + bonus_ref[slot_r]


def _wkv_direction(P, rev, B, T, L, RW, off, vecs, lvec, w2p, a2p):
    R = P.shape[0]
    C = CH
    RB = WKV_ROWS
    n_ctx, n_lat = L // RB, T // RB
    n_blk = n_ctx + n_lat
    lat0, ctx0 = 0, (B * T) // RB

    def block(b, i):
        if rev:
            return jnp.where(i < n_ctx, ctx0 + b * n_ctx + (n_ctx - 1 - i), lat0 + b * n_lat + (n_blk - 1 - i))
        return jnp.where(i < n_ctx, ctx0 + b * n_ctx + i, lat0 + b * n_lat + (i - n_ctx))

    t = jnp.arange(C)
    tri = (t[None, :] >= t[:, None]) if rev else (t[None, :] <= t[:, None])
    strict = (t[None, :] > t[:, None]) if rev else (t[None, :] < t[:, None])
    cm = jnp.concatenate([jnp.tile(strict, (1, 2)), jnp.tile(tri, (1, 2))], axis=0).astype(F32)
    g = jnp.arange(256) // RWKV_HEAD
    bd = (g[:, None] == g[None, :]).astype(BF16)
    n_pairs = RW // LANES
    col = lambda c0, w: (lambda b, i: (block(b, i), c0 // w))
    const = lambda b, i: (0, 0)
    slots = lambda rows, dt: pltpu.VMEM((1, rows, RW), dt)
    return pl.pallas_call(
        functools.partial(_wkv_kernel, rev, n_ctx),
        grid=(B, n_blk),
        in_specs=[pl.BlockSpec((RB, RW), col(off["r"], RW)),
                  pl.BlockSpec((RB, RW), col(off["k"], RW)),
                  pl.BlockSpec((RB, RW), col(off["v"], RW)),
                  pl.BlockSpec((RB, LANES), col(off["wd"], LANES)),
                  pl.BlockSpec((RB, LANES), col(off["ad"], LANES)),
                  pl.BlockSpec(vecs.shape, const),
                  pl.BlockSpec(lvec.shape, const),
                  pl.BlockSpec(w2p.shape, const),
                  pl.BlockSpec(a2p.shape, const),
                  pl.BlockSpec(bd.shape, const),
                  pl.BlockSpec((C, C), const),
                  pl.BlockSpec(cm.shape, const)],
        out_specs=pl.BlockSpec((RB, RW), lambda b, i: (block(b, i), 0)),
        out_shape=jax.ShapeDtypeStruct((R, RW), F32),
        scratch_shapes=[pltpu.VMEM((n_pairs, C, LANES), F32),
                        pltpu.VMEM((8, RW), F32), pltpu.VMEM((8, RW), F32), pltpu.VMEM((8, RW), F32),
                        pltpu.VMEM((8, LANES), F32), pltpu.VMEM((8, LANES), F32),
                        slots(RB, BF16), slots(RB, BF16), slots(RB, BF16), slots(RB, BF16),
                        slots(RB, BF16), slots(RB, BF16), slots(RB, BF16),
                        slots(8, F32), slots(RB, F32)],
        compiler_params=_cparams(("arbitrary", "arbitrary")),
        name="wkv_rev" if rev else "wkv_fwd",
    )(P, P, P, P, P, vecs, lvec, w2p, a2p, bd, tri.astype(BF16), cm)


def _residual_ln(x, y, gate, g, b, alpha):
    zz = alpha * x + gate * y
    mu = jnp.mean(zz, axis=-1, keepdims=True)
    d = zz - mu
    var = jnp.mean(d * d, axis=-1, keepdims=True)
    return d * lax.rsqrt(var + LN_EPS) * g + b


def _silu(g):
    return g * jax.nn.sigmoid(g)


def _sub_tiles(rows=BLK):
    return [slice(j * rows, (j + 1) * rows) for j in range(ROWS2 // rows)]


def _pool_mix(pos, seg_len, u_cur, u_prev, u_next, pw_ref):
    G = u_cur.shape[-1] // len(POOL_WINDOWS)
    n = u_cur.shape[0]
    t = pos + lax.broadcasted_iota(jnp.int32, (n, 1), 0)
    s_cur = pos + lax.broadcasted_iota(jnp.int32, (n, n), 1)
    s_prev = pos - HALO + lax.broadcasted_iota(jnp.int32, (n, HALO), 1)
    s_next = pos + n + lax.broadcasted_iota(jnp.int32, (n, HALO), 1)
    mixed = []
    for gi, win in enumerate(POOL_WINDOWS):
        lo = jnp.clip(t - win // 2, 0, seg_len)
        hi = jnp.clip(t - win // 2 + win, 0, seg_len)
        inv = 1.0 / (hi - lo).astype(F32)

        def band(sidx):
            return jnp.where(sidx >= lo, jnp.where(sidx < hi, 1.0, 0.0), 0.0).astype(BF16)

        cs = slice(gi * G, (gi + 1) * G)
        sums = (_dot(band(s_cur), u_cur[:, cs]) + _dot(band(s_prev), u_prev[:, cs])
                + _dot(band(s_next), u_next[:, cs]))
        pooled = sums * inv - u_cur[:, cs].astype(F32)
        mixed.append(_dot(pooled.astype(BF16), pw_ref[gi]))
    return jnp.concatenate(mixed, axis=1)


def _even_out_kernel(alpha, n_lat_rows, T, L,
                     x_ref, ac_ref, ap_ref, an_ref, gate_ref, yf_ref, yr_ref, mg_ref,
                     pw_ref, ps_ref, wo_ref, lg_ref, lb_ref, o_ref):
    m = pl.program_id(0)
    subs = _sub_tiles(OUT_SUB)
    acts = []
    for j, sub in enumerate(subs):
        row0 = m * ROWS2 + j * OUT_SUB
        is_lat = row0 < n_lat_rows
        seg_len = jnp.where(is_lat, T, L)
        pos = jnp.where(is_lat, row0 % T, (row0 - n_lat_rows) % L)
        u_prev = ap_ref[...] if j == 0 else ac_ref[j * OUT_SUB - HALO:j * OUT_SUB, :]
        u_next = (an_ref[...] if j == len(subs) - 1
                  else ac_ref[(j + 1) * OUT_SUB:(j + 1) * OUT_SUB + HALO, :])
        a_pool = _pool_mix(pos, seg_len, ac_ref[sub, :], u_prev, u_next, pw_ref) * ps_ref[...]
        act = (jnp.concatenate([a_pool, yf_ref[sub, :] + yr_ref[sub, :]], axis=1)
               * _silu(gate_ref[sub, :].astype(F32)))
        acts.append(act.astype(BF16))
    ys = [_dot(act, wo_ref[...]) for act in acts]
    for sub, y in zip(subs, ys):
        o_ref[sub, :] = _residual_ln(x_ref[sub, :], y, mg_ref[0], lg_ref[...], lb_ref[...], alpha)


def _odd_out_kernel(alpha, two_att, n_lat2, x_ref, *refs):
    if two_att:
        att_ref, attc_ref, gate_ref, mg_ref, wo_ref, lg_ref, lb_ref, o_ref = refs
        is_lat = pl.program_id(0) < n_lat2
    else:
        att_ref, gate_ref, mg_ref, wo_ref, lg_ref, lb_ref, o_ref = refs
    subs = _sub_tiles(OUT_SUB)
    acts = []
    for sub in subs:
        att = att_ref[sub, :]
        if two_att:
            att = jnp.where(is_lat, att, attc_ref[sub, :])
        acts.append((att.astype(F32) * _silu(gate_ref[sub, :].astype(F32))).astype(BF16))
    ys = [_dot(act, wo_ref[...]) for act in acts]
    for sub, y in zip(subs, ys):
        o_ref[sub, :] = _residual_ln(x_ref[sub, :], y, mg_ref[0], lg_ref[...], lb_ref[...], alpha)


def _mod_row(m, n_lat2, per_b, B):
    return jnp.where(m < n_lat2, m // per_b, B)


def _even_out(xa, P, yf, yr, mod3, pool_w, pool_scale, w_out, ln_g, ln_b, alpha, B, T, L, n_rows, off):
    R, D = xa.shape
    PW = pool_scale.shape[-1]
    n_lat2 = (B * T) // ROWS2
    per_b = T // ROWS2
    hb = ROWS2 // HALO
    last_halo = R // HALO - 1
    const2 = lambda m: (0, 0)
    return pl.pallas_call(
        functools.partial(_even_out_kernel, alpha, B * T, T, L),
        grid=(n_rows // ROWS2,),
        in_specs=[pl.BlockSpec((ROWS2, D), lambda m: (m, 0)),
                  pl.BlockSpec((ROWS2, PW), lambda m: (m, off["a"] // PW)),
                  pl.BlockSpec((HALO, PW), lambda m: (jnp.maximum(m * hb - 1, 0), off["a"] // PW)),
                  pl.BlockSpec((HALO, PW), lambda m: (jnp.minimum((m + 1) * hb, last_halo), off["a"] // PW)),
                  pl.BlockSpec((ROWS2, D), lambda m: (m, off["gate"] // D)),
                  pl.BlockSpec((ROWS2, D - PW), lambda m: (m, 0)),
                  pl.BlockSpec((ROWS2, D - PW), lambda m: (m, 0)),
                  pl.BlockSpec((1, 1, D), lambda m: (_mod_row(m, n_lat2, per_b, B), 0, 2)),
                  pl.BlockSpec(pool_w.shape, lambda m: (0, 0, 0)),
                  pl.BlockSpec((1, PW), const2),
                  pl.BlockSpec(w_out.shape, const2),
                  pl.BlockSpec((1, D), const2),
                  pl.BlockSpec((1, D), const2)],
        out_specs=pl.BlockSpec((ROWS2, D), lambda m: (m, 0)),
        out_shape=jax.ShapeDtypeStruct((n_rows, D), F32),
        compiler_params=_cparams(("arbitrary",)),
        name="even_out",
    )(xa, P, P, P, P, yf, yr, mod3, pool_w, pool_scale, w_out, ln_g, ln_b)


def _odd_out(xa, P, att, att_ctx, mod3, w_out, ln_g, ln_b, alpha, B, T, n_rows, off):
    R, D = xa.shape
    n_lat2 = (B * T) // ROWS2
    per_b = T // ROWS2
    const2 = lambda m: (0, 0)
    two_att = att_ctx is not None
    att_specs = [pl.BlockSpec((ROWS2, D), lambda m: (jnp.minimum(m, n_lat2 - 1), 0))]
    att_args = [att]
    if two_att:
        att_specs.append(pl.BlockSpec((ROWS2, D), lambda m: (jnp.maximum(m - n_lat2, 0), 0)))
        att_args.append(att_ctx)
    return pl.pallas_call(
        functools.partial(_odd_out_kernel, alpha, two_att, n_lat2),
        grid=(n_rows // ROWS2,),
        in_specs=[pl.BlockSpec((ROWS2, D), lambda m: (m, 0))] + att_specs + [
                  pl.BlockSpec((ROWS2, D), lambda m: (m, off["gate"] // D)),
                  pl.BlockSpec((1, 1, D), lambda m: (_mod_row(m, n_lat2, per_b, B), 0, 2)),
                  pl.BlockSpec(w_out.shape, const2),
                  pl.BlockSpec((1, D), const2),
                  pl.BlockSpec((1, D), const2)],
        out_specs=pl.BlockSpec((ROWS2, D), lambda m: (m, 0)),
        out_shape=jax.ShapeDtypeStruct((n_rows, D), F32),
        compiler_params=_cparams(("arbitrary",)),
        name="odd_out",
    )(xa, *att_args, P, mod3, w_out, ln_g, ln_b)


def _rope(xv, cos, sin):
    q = HEAD_DIM // 4
    lane = lax.broadcasted_iota(jnp.int32, (1, HEAD_DIM), 1)
    first_half = (lane % (2 * q)) < q
    partner = jnp.where(first_half, pltpu.roll(xv, HEAD_DIM - q, axis=1), pltpu.roll(xv, q, axis=1))
    return xv * cos + partner * sin


def _rms(xv, g):
    return xv * lax.rsqrt(jnp.mean(xv * xv, axis=-1, keepdims=True) + LN_EPS) * g


_Q_SCALE = HEAD_DIM ** -0.5 * LOG2_E


def _softmax_pv(s, v_ones):
    mx = jnp.max(s, axis=-1, keepdims=True)
    pr = jnp.exp2((s - mx).astype(BF16))
    o = _dot(pr, v_ones)
    return o[:, :HEAD_DIM] / o[:, HEAD_DIM:]


def _attn_kernel(q_ref, kc_ref, kl_ref, vc_ref, vl_ref, cq_ref, sq_ref, cos_ref, sin_ref, qg_ref, kg_ref,
                 o_ref, ks_ref, vs_ref):
    @pl.when(pl.program_id(2) == 0)
    def _():
        kf = jnp.concatenate([kc_ref[...], kl_ref[...]], axis=0).astype(F32)
        ks_ref[...] = _rope(_rms(kf, kg_ref[...]), cos_ref[...], sin_ref[...]).astype(BF16)
        vv = jnp.concatenate([vc_ref[...], vl_ref[...]], axis=0)
        vs_ref[...] = jnp.concatenate([vv, jnp.ones_like(vv)], axis=1)

    units = [(sub, h) for sub in _sub_tiles() for h in range(GROUP)]

    def scores(sub, h):
        qh = q_ref[sub, h * HEAD_DIM:(h + 1) * HEAD_DIM].astype(F32)
        qh = _rope(_rms(qh, qg_ref[...]), cq_ref[sub, :], sq_ref[sub, :]) * _Q_SCALE
        return _dot_nt(qh.astype(BF16), ks_ref[...])

    s_next = scores(*units[0])
    for n, (sub, h) in enumerate(units):
        s_cur = s_next
        if n + 1 < len(units):
            s_next = scores(*units[n + 1])
        o_ref[sub, h * HEAD_DIM:(h + 1) * HEAD_DIM] = _softmax_pv(s_cur, vs_ref[...]).astype(o_ref.dtype)


def _attn_ctx_kernel(q_ref, kc_ref, vc_ref, qg_ref, kg_ref, o_ref):
    kn = _rms(kc_ref[...].astype(F32), kg_ref[...]).astype(BF16)
    vv = vc_ref[...]
    v_ones = jnp.concatenate([vv, jnp.ones_like(vv)], axis=1)
    for h in range(GROUP):
        hs = slice(h * HEAD_DIM, (h + 1) * HEAD_DIM)
        qh = _rms(q_ref[:, hs].astype(F32), qg_ref[...]) * _Q_SCALE
        o_ref[:, hs] = _softmax_pv(_dot_nt(qh.astype(BF16), kn), v_ones).astype(o_ref.dtype)


def _attention(P, cos_t, sin_t, qg, kg, B, T, L, D, off):
    KVH = D // HEAD_DIM // GROUP
    QW = GROUP * HEAD_DIM
    per_b = T // ROWS2
    kcol = off["k"] // HEAD_DIM
    vcol = off["v"] // HEAD_DIM
    const = lambda b, g, i: (0, 0)
    return pl.pallas_call(
        _attn_kernel,
        grid=(B, KVH, per_b),
        in_specs=[pl.BlockSpec((ROWS2, QW), lambda b, g, i: (b * per_b + i, g)),
                  pl.BlockSpec((L, HEAD_DIM), lambda b, g, i: ((B * T) // L + b, kcol + g)),
                  pl.BlockSpec((T, HEAD_DIM), lambda b, g, i: (b, kcol + g)),
                  pl.BlockSpec((L, HEAD_DIM), lambda b, g, i: ((B * T) // L + b, vcol + g)),
                  pl.BlockSpec((T, HEAD_DIM), lambda b, g, i: (b, vcol + g)),
                  pl.BlockSpec((ROWS2, HEAD_DIM), lambda b, g, i: (i, 0)),
                  pl.BlockSpec((ROWS2, HEAD_DIM), lambda b, g, i: (i, 0)),
                  pl.BlockSpec((L + T, HEAD_DIM), const),
                  pl.BlockSpec((L + T, HEAD_DIM), const),
                  pl.BlockSpec((1, HEAD_DIM), const),
                  pl.BlockSpec((1, HEAD_DIM), const)],
        out_specs=pl.BlockSpec((ROWS2, QW), lambda b, g, i: (b * per_b + i, g)),
        out_shape=jax.ShapeDtypeStruct((B * T, D), BF16),
        scratch_shapes=[pltpu.VMEM((L + T, HEAD_DIM), BF16), pltpu.VMEM((L + T, 2 * HEAD_DIM), BF16)],
        compiler_params=_cparams(("arbitrary", "arbitrary", "arbitrary")),
        name="attention",
    )(P, P, P, P, P, cos_t[L:], sin_t[L:], cos_t, sin_t, qg, kg)


def _attention_ctx(P, qg, kg, B, T, L, D, off):
    KVH = D // HEAD_DIM // GROUP
    QW = GROUP * HEAD_DIM
    ctx0 = (B * T) // L
    kcol = off["k"] // HEAD_DIM
    vcol = off["v"] // HEAD_DIM
    const = lambda b, g: (0, 0)
    return pl.pallas_call(
        _attn_ctx_kernel,
        grid=(B, KVH),
        in_specs=[pl.BlockSpec((L, QW), lambda b, g: (ctx0 + b, g)),
                  pl.BlockSpec((L, HEAD_DIM), lambda b, g: (ctx0 + b, kcol + g)),
                  pl.BlockSpec((L, HEAD_DIM), lambda b, g: (ctx0 + b, vcol + g)),
                  pl.BlockSpec((1, HEAD_DIM), const),
                  pl.BlockSpec((1, HEAD_DIM), const)],
        out_specs=pl.BlockSpec((L, QW), lambda b, g: (b, g)),
        out_shape=jax.ShapeDtypeStruct((B * L, D), BF16),
        compiler_params=_cparams(("arbitrary", "arbitrary")),
        name="attention_ctx",
    )(P, P, P, qg, kg)


def _rope_tables(T, L):
    q = HEAD_DIM // 4
    rows = T // GRID_W
    row = jnp.repeat(jnp.arange(rows, dtype=F32), GRID_W)
    col = jnp.tile(jnp.arange(GRID_W, dtype=F32), rows)
    inv_freq = ROPE_THETA ** (-jnp.arange(0, 2 * q, 2, dtype=F32) / (2 * q))
    ar, ac = row[:, None] * inv_freq, col[:, None] * inv_freq
    cos = jnp.concatenate([jnp.cos(ar), jnp.cos(ar), jnp.cos(ac), jnp.cos(ac)], axis=1)
    sin = jnp.concatenate([-jnp.sin(ar), jnp.sin(ar), -jnp.sin(ac), jnp.sin(ac)], axis=1)
    cos = jnp.concatenate([jnp.ones((L, HEAD_DIM), F32), cos], axis=0)
    sin = jnp.concatenate([jnp.zeros((L, HEAD_DIM), F32), sin], axis=0)
    return cos, sin


def _even_layer(xa, mod3, last, B, T, L, alpha, w_in, w_out, pool_w, pool_scale, mu_rkv, mu_lora, w0, w2,
                a0, a2, k_k, k_a, r_k, gn_g, gn_b, ln_g, ln_b):
    R, D = xa.shape
    PW = pool_scale.shape[-1]
    RW = k_k.shape[-1]
    n_split = PW + 3 * RW
    lora2 = 2 * LORA_W
    w = jnp.concatenate([w_in[:, :n_split], w_in[:, n_split + 2 * lora2:],
                         w_in[:, n_split:n_split + 2 * lora2]], axis=1).astype(BF16)
    off = {"a": 0, "r": PW, "k": PW + RW, "v": PW + 2 * RW, "gate": n_split,
           "wd": n_split + PW + RW, "ad": n_split + PW + RW + lora2}
    P = _projection(xa, mod3, w, B, T, _proj_tn(w.shape[1]))

    ys = []
    for d, rev in enumerate((False, True)):
        vecs = jnp.zeros((16, RW), F32)
        rows = [mu_rkv[d, 0], mu_rkv[d, 1], mu_rkv[d, 2], w0[d], a0[d], k_k, k_a, r_k.reshape(RW),
                gn_g[d], gn_b[d]]
        vecs = vecs.at[:len(rows)].set(jnp.stack(rows))
        lvec = jnp.zeros((8, lora2), F32).at[0].set(mu_lora[:, 0, :].reshape(lora2)).at[1].set(
            mu_lora[:, 1, :].reshape(lora2))
        w2p = jnp.zeros((lora2, RW), F32).at[d * LORA_W:(d + 1) * LORA_W].set(w2[d]).astype(BF16)
        a2p = jnp.zeros((lora2, RW), F32).at[d * LORA_W:(d + 1) * LORA_W].set(a2[d]).astype(BF16)
        ys.append(_wkv_direction(P, rev, B, T, L, RW, off, vecs, lvec, w2p, a2p))

    n_rows = B * T if last else R
    return _even_out(xa, P, ys[0], ys[1], mod3, pool_w.astype(BF16), pool_scale.reshape(1, PW),
                     w_out.astype(BF16), ln_g.reshape(1, D), ln_b.reshape(1, D), alpha, B, T, L, n_rows, off)


def _odd_layer(xa, mod3, last, B, T, L, alpha, w_in, w_out, q_g, k_g, tables, ln_g, ln_b):
    R, D = xa.shape
    KVW = D // GROUP
    w = jnp.concatenate([w_in[:, :D], w_in[:, D + 2 * KVW:], w_in[:, D:D + 2 * KVW]], axis=1).astype(BF16)
    off = {"q": 0, "gate": D, "k": 2 * D, "v": 2 * D + KVW}
    P = _projection(xa, mod3, w, B, T, _proj_tn(w.shape[1]))
    qg, kg = q_g.reshape(1, HEAD_DIM), k_g.reshape(1, HEAD_DIM)
    att = _attention(P, tables[0], tables[1], qg, kg, B, T, L, D, off)
    att_ctx = None if last else _attention_ctx(P, qg, kg, B, T, L, D, off)
    n_rows = B * T if last else R
    return _odd_out(xa, P, att, att_ctx, mod3, w_out.astype(BF16), ln_g.reshape(1, D), ln_b.reshape(1, D),
                    alpha, B, T, n_rows, off)


def kernel(x, c, ctx, c_ctx, mod_w, mod_b, ln_g, ln_b, ev_w_in, ev_w_out, pool_w, pool_scale, rw_mu_rkv,
           rw_mu_lora, rw_w0, rw_w2, rw_a0, rw_a2, rw_k_k, rw_k_a, rw_r_k, rw_gn_g, rw_gn_b, od_w_in,
           od_w_out, q_norm_g, k_norm_g):
    B, T, D = x.shape
    L = ctx.shape[1]
    depth = mod_w.shape[0]
    assert L == BLK and T % TM == 0 and (B * L) % TM == 0 and T % GRID_W == 0 and B < MOD_ROWS
    assert D // HEAD_DIM % GROUP == 0 and (D // 2) % 256 == 0 and L % WKV_ROWS == 0 and 2 * CH == LANES
    alpha = (2 * depth) ** 0.25
    xa = jnp.concatenate([x.reshape(B * T, D), ctx.reshape(B * L, D)], axis=0)
    mod = _modulation(c, c_ctx, mod_w, mod_b)
    tables = _rope_tables(T, L)
    for l in range(depth):
        last = l == depth - 1
        i = l // 2
        mod3 = mod[l].reshape(MOD_ROWS, 1, 3 * D)
        if l % 2 == 0:
            xa = _even_layer(xa, mod3, last, B, T, L, alpha, ev_w_in[i], ev_w_out[i], pool_w[i], pool_scale[i],
                             rw_mu_rkv[i], rw_mu_lora[i], rw_w0[i], rw_w2[i], rw_a0[i], rw_a2[i], rw_k_k[i],
                             rw_k_a[i], rw_r_k[i], rw_gn_g[i], rw_gn_b[i], ln_g[l], ln_b[l])
        else:
            xa = _odd_layer(xa, mod3, last, B, T, L, alpha, od_w_in[i], od_w_out[i], q_norm_g[i], k_norm_g[i],
                            tables, ln_g[l], ln_b[l])
    return xa[:B * T].reshape(B, T, D)
```

```python
import functools

import jax
import jax.numpy as jnp
from jax import lax
from jax.experimental import pallas as pl
from jax.experimental.pallas import tpu as pltpu

F32 = jnp.float32
BF16 = jnp.bfloat16

LANES = 128
VMEM_LIMIT = 52 * 1024 * 1024

GRID_W = 64
POOL_WINDOWS = (2, 4, 8, 16)
RWKV_HEAD = 64
LORA_W = 64
GN_EPS = 64e-5
HEAD_DIM = 128
GROUP = 4
ROPE_THETA = 10000.0
LN_EPS = 1e-6
LOG2_E = 1.4426950408889634
EXP_NEG_HALF = 0.6065306597126334

MOD_ROWS = 24
TM = 1024
BLK = 256
ROWS2 = 2 * BLK
ATT_ROWS = 4 * BLK
OUT_SUB = 256
CH = 64
WKV_ROWS = 256
HALO = 16


def _dot(a, b):
    return jnp.dot(a, b, preferred_element_type=F32)


def _dot_nt(a, b):
    return lax.dot_general(a, b, (((1,), (1,)), ((), ())), preferred_element_type=F32)


def _dot_tn(a, b):
    return lax.dot_general(a, b, (((0,), (0,)), ((), ())), preferred_element_type=F32)


def _cparams(sem):
    return pltpu.CompilerParams(dimension_semantics=sem, vmem_limit_bytes=VMEM_LIMIT)


def _mod_kernel(c_ref, w_ref, b_ref, o_ref):
    cc = c_ref[...]
    s = cc * jax.nn.sigmoid(cc)
    o_ref[0] = _dot(s.astype(BF16), w_ref[0].astype(BF16)) + b_ref[0]


def _modulation(c, c_ctx, mod_w, mod_b):
    depth, D, D3 = mod_w.shape
    B = c.shape[0]
    cc = jnp.zeros((MOD_ROWS, D), F32).at[:B].set(c).at[B].set(c_ctx)
    tn = 1024
    return pl.pallas_call(
        _mod_kernel,
        grid=(depth, D3 // tn),
        in_specs=[pl.BlockSpec((MOD_ROWS, D), lambda l, n: (0, 0)),
                  pl.BlockSpec((1, D, tn), lambda l, n: (l, 0, n)),
                  pl.BlockSpec((1, 1, tn), lambda l, n: (l, 0, n))],
        out_specs=pl.BlockSpec((1, MOD_ROWS, tn), lambda l, n: (l, 0, n)),
        out_shape=jax.ShapeDtypeStruct((depth, MOD_ROWS, D3), F32),
        compiler_params=_cparams(("arbitrary", "arbitrary")),
        name="modulation",
    )(cc, mod_w, mod_b.reshape(depth, 1, D3))


def _proj_kernel(x_ref, sh_ref, sc_ref, w_ref, o_ref, h_ref):
    @pl.when(pl.program_id(1) == 0)
    def _():
        h_ref[...] = (x_ref[...] * (1.0 + sc_ref[0]) + sh_ref[0]).astype(BF16)

    o_ref[...] = _dot(h_ref[...], w_ref[...]).astype(o_ref.dtype)


def _proj_tn(n):
    return next(t for t in (1280, 1024, 768, 512, 256) if n % t == 0)


def _projection(xa, mod3, w, B, T, tn):
    R, D = xa.shape
    N = w.shape[1]
    n_lat = (B * T) // TM
    per_b = T // TM

    def mrow(m):
        return jnp.where(m < n_lat, m // per_b, B)

    return pl.pallas_call(
        _proj_kernel,
        grid=(R // TM, N // tn),
        in_specs=[pl.BlockSpec((TM, D), lambda m, n: (m, 0)),
                  pl.BlockSpec((1, 1, D), lambda m, n: (mrow(m), 0, 0)),
                  pl.BlockSpec((1, 1, D), lambda m, n: (mrow(m), 0, 1)),
                  pl.BlockSpec((D, tn), lambda m, n: (0, n))],
        out_specs=pl.BlockSpec((TM, tn), lambda m, n: (m, n)),
        out_shape=jax.ShapeDtypeStruct((R, N), BF16),
        scratch_shapes=[pltpu.VMEM((TM, D), BF16)],
        compiler_params=_cparams(("arbitrary", "arbitrary")),
        name="in_proj",
    )(xa, mod3, mod3, w)


def _wkv_kernel(rev, n_ctx_blk,
                r0_ref, k0_ref, v0_ref, wd_ref, ad_ref, vec_ref, lvec_ref, w2_ref, a2_ref,
                bd_ref, tri_ref, cm_ref,
                o_ref,
                h_ref, cr_ref, ck_ref, cv_ref, cw_ref, ca_ref,
                at_ref, rt_ref, bt_ref, kt_ref, bh_ref, kh_ref, vv_ref, ge_ref, bonus_ref):
    i = pl.program_id(1)
    C = CH
    RB, W = r0_ref.shape
    n_pairs = W // LANES
    n_ch = RB // C

    @pl.when(i == 0)
    def _():
        h_ref[...] = jnp.zeros_like(h_ref)
        for c_ref in (cr_ref, ck_ref, cv_ref, cw_ref, ca_ref):
            c_ref[...] = jnp.zeros_like(c_ref)

    seg_start = jnp.logical_or(i == 0, i == n_ctx_blk)
    SUB = 8
    edge = lax.broadcasted_iota(jnp.int32, (SUB, 1), 0) == (SUB - 1 if rev else 0)

    def tshift(x_ref, c_ref, mu):
        xv = x_ref[...].astype(F32)
        nb = pltpu.roll(xv, (RB - 1) if rev else 1, axis=0)
        prev = jnp.where(seg_start, 0.0, c_ref[0:1, :])
        if rev:
            nb = jnp.concatenate([nb[:RB - SUB], jnp.where(edge, prev, nb[RB - SUB:])], axis=0)
        else:
            nb = jnp.concatenate([jnp.where(edge, prev, nb[:SUB]), nb[SUB:]], axis=0)
        c_ref[0:1, :] = xv[0:1, :] if rev else xv[RB - 1:RB, :]
        return xv + (nb - xv) * mu

    def seg(xv):
        parts = [_dot(xv[:, g * 256:(g + 1) * 256].astype(BF16), bd_ref[...]) for g in range(W // 256)]
        return jnp.concatenate(parts, axis=1)

    vec = lambda j: vec_ref[j:j + 1, :]
    chunks = range(n_ch)
    rows = [slice(c * C, (c + 1) * C) for c in chunks]

    def prepare_shift():
        r = tshift(r0_ref, cr_ref, vec(0))
        k = tshift(k0_ref, ck_ref, vec(1))
        v = tshift(v0_ref, cv_ref, vec(2))
        wdl = jnp.tanh(tshift(wd_ref, cw_ref, lvec_ref[0:1, :]))
        w_pre = vec(3) + _dot(wdl.astype(BF16), w2_ref[...])
        adl = tshift(ad_ref, ca_ref, lvec_ref[1:2, :])
        a_pre = vec(4) + _dot(adl.astype(BF16), a2_ref[...])
        kk0 = k * vec(5)
        return r, k, v, w_pre, a_pre, kk0, seg(kk0 * kk0)

    def prepare_decay(r, k, v, w_pre, a_pre, kk0, kk_norm):
        a = jax.nn.sigmoid(a_pre)
        lw = -EXP_NEG_HALF * jax.nn.sigmoid(w_pre)
        kk = kk0 * lax.rsqrt(kk_norm + 1e-12)
        k2 = k * (1.0 + (a - 1.0) * vec(6))
        b_in = kk * a
        bonus_ref[...] = seg(r * k2 * vec(7)) * v
        vv_ref[...] = v.astype(BF16)
        lw_hi = lw.astype(BF16)
        lw_lo = (lw - lw_hi.astype(F32)).astype(BF16)
        for c in chunks:
            rs = rows[c]
            cum = _dot(tri_ref[...], lw_hi[rs]) + _dot(tri_ref[...], lw_lo[rs])
            e_in = jnp.exp(cum)
            e_out = jnp.exp(-cum)
            g_end = jnp.exp(cum[0:1, :] if rev else cum[C - 1:C, :])
            bt = b_in[rs] * e_out
            kt = k2[rs] * e_out
            at_ref[rs, :] = (-kk[rs] * jnp.exp(cum - lw[rs])).astype(BF16)
            rt_ref[rs, :] = (r[rs] * e_in).astype(BF16)
            bt_ref[rs, :] = bt.astype(BF16)
            kt_ref[rs, :] = kt.astype(BF16)
            bh_ref[rs, :] = (bt * g_end).astype(BF16)
            kh_ref[rs, :] = (kt * g_end).astype(BF16)
            ge_ref[c:c + 1, :] = g_end

    prepare_decay(*prepare_shift())

    lane = lax.broadcasted_iota(jnp.int32, (1, LANES), 1)
    lo = lane < RWKV_HEAD
    lane2 = lax.broadcasted_iota(jnp.int32, (1, 2 * LANES), 1)
    lo2 = (lane2 % LANES) < RWKV_HEAD
    cm = cm_ref[...] > 0.5
    t_idx = lax.broadcasted_iota(jnp.int32, (C, LANES), 0)
    j_idx = lax.broadcasted_iota(jnp.int32, (C, LANES), 1) % C
    on_diag = t_idx == j_idx
    eye_c = on_diag.astype(F32)
    same_block = [(t_idx >> l) == (j_idx >> l) for l in range(C.bit_length())]
    off_block = [jnp.logical_and(same_block[l + 1], jnp.logical_not(same_block[l]))
                 for l in range(C.bit_length() - 1)]
    zeros_c = jnp.zeros((C, LANES), BF16)

    def first(xv, m=lo):
        return jnp.where(m, xv, jnp.zeros_like(xv))

    def second(xv, m=lo):
        return jnp.where(m, jnp.zeros_like(xv), xv)

    def bdiag(xv, m):
        xb = xv.astype(BF16)
        return jnp.concatenate([first(xb, m), second(xb, m)], axis=0)

    col = [slice(p * LANES, (p + 1) * LANES) for p in range(n_pairs)]

    def state_free(group):
        units = [(c, p) for c in group for p in range(n_pairs)]
        ld = lambda ref: {(c, p): ref[rows[c], col[p]] for c, p in units}
        at_s, rt_s, bt_s, kt_s, bh_s, kh_s, v_s = (ld(ref) for ref in (at_ref, rt_ref, bt_ref, kt_ref,
                                                                      bh_ref, kh_ref, vv_ref))
        g_end_s = {(c, p): ge_ref[c:c + 1, col[p]] for c, p in units}
        g0 = {u: jnp.where(cm, _dot_nt(jnp.concatenate([first(at_s[u]), first(rt_s[u])], axis=0),
                                       jnp.concatenate([bt_s[u], kt_s[u]], axis=0)), 0.0)
              for u in units}
        g1 = {u: jnp.where(cm, _dot_nt(jnp.concatenate([second(at_s[u]), second(rt_s[u])], axis=0),
                                       jnp.concatenate([kt_s[u], bt_s[u]], axis=0)), 0.0)
              for u in units}
        pcat = {u: jnp.where(lo, g0[u][:C], g1[u][:C]) for u in units}
        aksw = {u: jnp.where(lo, g1[u][:C], g0[u][:C]) for u in units}
        rbcat = {u: jnp.where(lo, g0[u][C:], g1[u][C:]) for u in units}
        rksw = {u: jnp.where(lo, g1[u][C:], g0[u][C:]) for u in units}
        akv = {u: _dot(aksw[u].astype(BF16), jnp.concatenate([second(v_s[u]), first(v_s[u])], axis=0))
               for u in units}

        tacc = {u: eye_c + jnp.where(same_block[1], pcat[u], 0.0) for u in units}
        for lvl in range(1, C.bit_length() - 1):
            off = {u: jnp.where(off_block[lvl], pcat[u], 0.0) for u in units}
            xo = {u: _dot(off[u].astype(BF16), bdiag(tacc[u], lo)) for u in units}
            tacc = {u: tacc[u] + _dot(tacc[u].astype(BF16), bdiag(xo[u], lo)) for u in units}
        av = {u: _dot(tacc[u].astype(BF16),
                      bdiag(jnp.concatenate([at_s[u], akv[u].astype(BF16)], axis=1), lo2))
              for u in units}

        rhs = {u: jnp.concatenate([
            bdiag(av[u], lo2),
            jnp.concatenate([zeros_c, second(v_s[u])], axis=1),
            jnp.concatenate([zeros_c, first(v_s[u])], axis=1)], axis=0) for u in units}
        out_c = {u: _dot(jnp.concatenate([rbcat[u], rksw[u]], axis=1).astype(BF16), rhs[u])
                 for u in units}
        out_d = {u: _dot_tn(jnp.concatenate([first(bh_s[u]), second(bh_s[u]), second(kh_s[u]),
                                             first(kh_s[u])], axis=0), rhs[u]) for u in units}
        md = {u: out_d[u][:C] + out_d[u][C:] for u in units}
        lhs_b = {u: jnp.concatenate([md[u][:, :LANES] + jnp.where(on_diag, g_end_s[u], 0.0),
                                     rt_s[u].astype(F32) + out_c[u][:, :LANES]], axis=0).astype(BF16)
                 for u in units}
        return lhs_b, {u: md[u][:, LANES:] for u in units}, out_c

    order = list(reversed(chunks)) if rev else list(chunks)
    lhs_b, n_cat, out_c = state_free(order)

    pairs = range(n_pairs)
    h = [h_ref[p] for p in pairs]
    y_rows = [None] * n_ch
    for c in order:
        ob = [_dot(lhs_b[c, p], bdiag(h[p], lo)) for p in pairs]
        h = [ob[p][:C] + n_cat[c, p] for p in pairs]
        y_rows[c] = jnp.concatenate([ob[p][C:] + out_c[c, p][:, LANES:] for p in pairs], axis=1)
    for p in pairs:
        h_ref[p] = h[p]
    y = jnp.concatenate(y_rows, axis=0)
    inv_n = 1.0 / RWKV_HEAD
    mu = seg(y) * inv_n
    d = y - mu
    var = seg(d * d) * inv_n
    o_ref[...] = d * lax.rsqrt(var + GN_EPS) * vec(8) + vec(9) + bonus_ref[...]


def _wkv_direction(P, rev, B, T, L, RW, off, vecs, lvec, w2p, a2p):
    R = P.shape[0]
    C = CH
    RB = WKV_ROWS
    n_ctx, n_lat = L // RB, T // RB
    n_blk = n_ctx + n_lat
    lat0, ctx0 = 0, (B * T) // RB

    def block(b, i):
        if rev:
            return jnp.where(i < n_ctx, ctx0 + b * n_ctx + (n_ctx - 1 - i), lat0 + b * n_lat + (n_blk - 1 - i))
        return jnp.where(i < n_ctx, ctx0 + b * n_ctx + i, lat0 + b * n_lat + (i - n_ctx))

    t = jnp.arange(C)
    tri = (t[None, :] >= t[:, None]) if rev else (t[None, :] <= t[:, None])
    strict = (t[None, :] > t[:, None]) if rev else (t[None, :] < t[:, None])
    cm = jnp.concatenate([jnp.tile(strict, (1, 2)), jnp.tile(tri, (1, 2))], axis=0).astype(F32)
    g = jnp.arange(256) // RWKV_HEAD
    bd = (g[:, None] == g[None, :]).astype(BF16)
    n_pairs = RW // LANES
    col = lambda c0, w: (lambda b, i: (block(b, i), c0 // w))
    const = lambda b, i: (0, 0)
    staged = lambda rows, dt: pltpu.VMEM((rows, RW), dt)
    return pl.pallas_call(
        functools.partial(_wkv_kernel, rev, n_ctx),
        grid=(B, n_blk),
        in_specs=[pl.BlockSpec((RB, RW), col(off["r"], RW)),
                  pl.BlockSpec((RB, RW), col(off["k"], RW)),
                  pl.BlockSpec((RB, RW), col(off["v"], RW)),
                  pl.BlockSpec((RB, LANES), col(off["wd"], LANES)),
                  pl.BlockSpec((RB, LANES), col(off["ad"], LANES)),
                  pl.BlockSpec(vecs.shape, const),
                  pl.BlockSpec(lvec.shape, const),
                  pl.BlockSpec(w2p.shape, const),
                  pl.BlockSpec(a2p.shape, const),
                  pl.BlockSpec(bd.shape, const),
                  pl.BlockSpec((C, C), const),
                  pl.BlockSpec(cm.shape, const)],
        out_specs=pl.BlockSpec((RB, RW), lambda b, i: (block(b, i), 0)),
        out_shape=jax.ShapeDtypeStruct((R, RW), F32),
        scratch_shapes=[pltpu.VMEM((n_pairs, C, LANES), F32),
                        pltpu.VMEM((8, RW), F32), pltpu.VMEM((8, RW), F32), pltpu.VMEM((8, RW), F32),
                        pltpu.VMEM((8, LANES), F32), pltpu.VMEM((8, LANES), F32),
                        staged(RB, BF16), staged(RB, BF16), staged(RB, BF16), staged(RB, BF16),
                        staged(RB, BF16), staged(RB, BF16), staged(RB, BF16),
                        staged(8, F32), staged(RB, F32)],
        compiler_params=_cparams(("arbitrary", "arbitrary")),
        name="wkv_rev" if rev else "wkv_fwd",
    )(P, P, P, P, P, vecs, lvec, w2p, a2p, bd, tri.astype(BF16), cm)


def _residual_ln(x, y, gate, g, b, alpha):
    zz = alpha * x + gate * y
    mu = jnp.mean(zz, axis=-1, keepdims=True)
    d = zz - mu
    var = jnp.mean(d * d, axis=-1, keepdims=True)
    return d * lax.rsqrt(var + LN_EPS) * g + b


def _silu(g):
    return g * jax.nn.sigmoid(g)


def _sub_tiles(rows=BLK, total=ROWS2):
    return [slice(j * rows, (j + 1) * rows) for j in range(total // rows)]


def _pool_mix(pos, seg_len, u_cur, u_prev, u_next, pw_ref):
    G = u_cur.shape[-1] // len(POOL_WINDOWS)
    n = u_cur.shape[0]
    t = pos + lax.broadcasted_iota(jnp.int32, (n, 1), 0)
    s_cur = pos + lax.broadcasted_iota(jnp.int32, (n, n), 1)
    s_prev = pos - HALO + lax.broadcasted_iota(jnp.int32, (n, HALO), 1)
    s_next = pos + n + lax.broadcasted_iota(jnp.int32, (n, HALO), 1)
    mixed = []
    for gi, win in enumerate(POOL_WINDOWS):
        lo = jnp.clip(t - win // 2, 0, seg_len)
        hi = jnp.clip(t - win // 2 + win, 0, seg_len)
        inv = 1.0 / (hi - lo).astype(F32)

        def band(sidx):
            return jnp.where(sidx >= lo, jnp.where(sidx < hi, 1.0, 0.0), 0.0).astype(BF16)

        cs = slice(gi * G, (gi + 1) * G)
        sums = (_dot(band(s_cur), u_cur[:, cs]) + _dot(band(s_prev), u_prev[:, cs])
                + _dot(band(s_next), u_next[:, cs]))
        pooled = sums * inv - u_cur[:, cs].astype(F32)
        mixed.append(_dot(pooled.astype(BF16), pw_ref[gi]))
    return jnp.concatenate(mixed, axis=1)


def _even_out_kernel(alpha, n_lat_rows, T, L,
                     x_ref, ac_ref, ap_ref, an_ref, gate_ref, yf_ref, yr_ref, mg_ref,
                     pw_ref, ps_ref, wo_ref, lg_ref, lb_ref, o_ref):
    m = pl.program_id(0)
    subs = _sub_tiles(OUT_SUB)
    acts = []
    for j, sub in enumerate(subs):
        row0 = m * ROWS2 + j * OUT_SUB
        is_lat = row0 < n_lat_rows
        seg_len = jnp.where(is_lat, T, L)
        pos = jnp.where(is_lat, row0 % T, (row0 - n_lat_rows) % L)
        u_prev = ap_ref[...] if j == 0 else ac_ref[j * OUT_SUB - HALO:j * OUT_SUB, :]
        u_next = (an_ref[...] if j == len(subs) - 1
                  else ac_ref[(j + 1) * OUT_SUB:(j + 1) * OUT_SUB + HALO, :])
        a_pool = _pool_mix(pos, seg_len, ac_ref[sub, :], u_prev, u_next, pw_ref) * ps_ref[...]
        act = (jnp.concatenate([a_pool, yf_ref[sub, :] + yr_ref[sub, :]], axis=1)
               * _silu(gate_ref[sub, :].astype(F32)))
        acts.append(act.astype(BF16))
    ys = [_dot(act, wo_ref[...]) for act in acts]
    for sub, y in zip(subs, ys):
        o_ref[sub, :] = _residual_ln(x_ref[sub, :], y, mg_ref[0], lg_ref[...], lb_ref[...], alpha)


def _odd_out_kernel(alpha, two_att, n_lat2, x_ref, *refs):
    if two_att:
        att_ref, attc_ref, gate_ref, mg_ref, wo_ref, lg_ref, lb_ref, o_ref = refs
        is_lat = pl.program_id(0) < n_lat2
    else:
        att_ref, gate_ref, mg_ref, wo_ref, lg_ref, lb_ref, o_ref = refs
    subs = _sub_tiles(OUT_SUB)
    acts = []
    for sub in subs:
        att = att_ref[sub, :]
        if two_att:
            att = jnp.where(is_lat, att, attc_ref[sub, :])
        acts.append((att.astype(F32) * _silu(gate_ref[sub, :].astype(F32))).astype(BF16))
    ys = [_dot(act, wo_ref[...]) for act in acts]
    for sub, y in zip(subs, ys):
        o_ref[sub, :] = _residual_ln(x_ref[sub, :], y, mg_ref[0], lg_ref[...], lb_ref[...], alpha)


def _mod_row(m, n_lat2, per_b, B):
    return jnp.where(m < n_lat2, m // per_b, B)


def _even_out(xa, P, yf, yr, mod3, pool_w, pool_scale, w_out, ln_g, ln_b, alpha, B, T, L, n_rows, off):
    R, D = xa.shape
    PW = pool_scale.shape[-1]
    n_lat2 = (B * T) // ROWS2
    per_b = T // ROWS2
    hb = ROWS2 // HALO
    last_halo = R // HALO - 1
    const2 = lambda m: (0, 0)
    return pl.pallas_call(
        functools.partial(_even_out_kernel, alpha, B * T, T, L),
        grid=(n_rows // ROWS2,),
        in_specs=[pl.BlockSpec((ROWS2, D), lambda m: (m, 0)),
                  pl.BlockSpec((ROWS2, PW), lambda m: (m, off["a"] // PW)),
                  pl.BlockSpec((HALO, PW), lambda m: (jnp.maximum(m * hb - 1, 0), off["a"] // PW)),
                  pl.BlockSpec((HALO, PW), lambda m: (jnp.minimum((m + 1) * hb, last_halo), off["a"] // PW)),
                  pl.BlockSpec((ROWS2, D), lambda m: (m, off["gate"] // D)),
                  pl.BlockSpec((ROWS2, D - PW), lambda m: (m, 0)),
                  pl.BlockSpec((ROWS2, D - PW), lambda m: (m, 0)),
                  pl.BlockSpec((1, 1, D), lambda m: (_mod_row(m, n_lat2, per_b, B), 0, 2)),
                  pl.BlockSpec(pool_w.shape, lambda m: (0, 0, 0)),
                  pl.BlockSpec((1, PW), const2),
                  pl.BlockSpec(w_out.shape, const2),
                  pl.BlockSpec((1, D), const2),
                  pl.BlockSpec((1, D), const2)],
        out_specs=pl.BlockSpec((ROWS2, D), lambda m: (m, 0)),
        out_shape=jax.ShapeDtypeStruct((n_rows, D), F32),
        compiler_params=_cparams(("arbitrary",)),
        name="even_out",
    )(xa, P, P, P, P, yf, yr, mod3, pool_w, pool_scale, w_out, ln_g, ln_b)


def _odd_out(xa, P, att, att_ctx, mod3, w_out, ln_g, ln_b, alpha, B, T, n_rows, off):
    R, D = xa.shape
    n_lat2 = (B * T) // ROWS2
    per_b = T // ROWS2
    const2 = lambda m: (0, 0)
    two_att = att_ctx is not None
    att_specs = [pl.BlockSpec((ROWS2, D), lambda m: (jnp.minimum(m, n_lat2 - 1), 0))]
    att_args = [att]
    if two_att:
        att_specs.append(pl.BlockSpec((ROWS2, D), lambda m: (jnp.maximum(m - n_lat2, 0), 0)))
        att_args.append(att_ctx)
    return pl.pallas_call(
        functools.partial(_odd_out_kernel, alpha, two_att, n_lat2),
        grid=(n_rows // ROWS2,),
        in_specs=[pl.BlockSpec((ROWS2, D), lambda m: (m, 0))] + att_specs + [
                  pl.BlockSpec((ROWS2, D), lambda m: (m, off["gate"] // D)),
                  pl.BlockSpec((1, 1, D), lambda m: (_mod_row(m, n_lat2, per_b, B), 0, 2)),
                  pl.BlockSpec(w_out.shape, const2),
                  pl.BlockSpec((1, D), const2),
                  pl.BlockSpec((1, D), const2)],
        out_specs=pl.BlockSpec((ROWS2, D), lambda m: (m, 0)),
        out_shape=jax.ShapeDtypeStruct((n_rows, D), F32),
        compiler_params=_cparams(("arbitrary",)),
        name="odd_out",
    )(xa, *att_args, P, mod3, w_out, ln_g, ln_b)


def _rope(xv, cos, sin):
    q = HEAD_DIM // 4
    lane = lax.broadcasted_iota(jnp.int32, (1, HEAD_DIM), 1)
    first_half = (lane % (2 * q)) < q
    partner = jnp.where(first_half, pltpu.roll(xv, HEAD_DIM - q, axis=1), pltpu.roll(xv, q, axis=1))
    return xv * cos + partner * sin


def _rms(xv, g):
    return xv * lax.rsqrt(jnp.mean(xv * xv, axis=-1, keepdims=True) + LN_EPS) * g


_Q_SCALE = HEAD_DIM ** -0.5 * LOG2_E


def _softmax_pv(s, v_ones):
    mx = jnp.max(s, axis=-1, keepdims=True)
    pr = jnp.exp2((s - mx).astype(BF16))
    o = _dot(pr, v_ones)
    return o[:, :HEAD_DIM] / o[:, HEAD_DIM:]


def _attn_kernel(q_ref, kc_ref, kl_ref, vc_ref, vl_ref, cq_ref, sq_ref, cos_ref, sin_ref, qg_ref, kg_ref,
                 o_ref, ks_ref, vs_ref):
    @pl.when(pl.program_id(2) == 0)
    def _():
        kf = jnp.concatenate([kc_ref[...], kl_ref[...]], axis=0).astype(F32)
        ks_ref[...] = _rope(_rms(kf, kg_ref[...]), cos_ref[...], sin_ref[...]).astype(BF16)
        vv = jnp.concatenate([vc_ref[...], vl_ref[...]], axis=0)
        vs_ref[...] = jnp.concatenate([vv, jnp.ones_like(vv)], axis=1)

    units = [(sub, h) for sub in _sub_tiles(BLK, ATT_ROWS) for h in range(GROUP)]

    def scores(sub, h):
        qh = q_ref[sub, h * HEAD_DIM:(h + 1) * HEAD_DIM].astype(F32)
        qh = _rope(_rms(qh, qg_ref[...]), cq_ref[sub, :], sq_ref[sub, :]) * _Q_SCALE
        return _dot_nt(qh.astype(BF16), ks_ref[...])

    s_next = scores(*units[0])
    for n, (sub, h) in enumerate(units):
        s_cur = s_next
        if n + 1 < len(units):
            s_next = scores(*units[n + 1])
        o_ref[sub, h * HEAD_DIM:(h + 1) * HEAD_DIM] = _softmax_pv(s_cur, vs_ref[...]).astype(o_ref.dtype)


def _attn_ctx_kernel(q_ref, kc_ref, vc_ref, qg_ref, kg_ref, o_ref):
    kn = _rms(kc_ref[...].astype(F32), kg_ref[...]).astype(BF16)
    vv = vc_ref[...]
    v_ones = jnp.concatenate([vv, jnp.ones_like(vv)], axis=1)
    for h in range(GROUP):
        hs = slice(h * HEAD_DIM, (h + 1) * HEAD_DIM)
        qh = _rms(q_ref[:, hs].astype(F32), qg_ref[...]) * _Q_SCALE
        o_ref[:, hs] = _softmax_pv(_dot_nt(qh.astype(BF16), kn), v_ones).astype(o_ref.dtype)


def _attention(P, cos_t, sin_t, qg, kg, B, T, L, D, off):
    KVH = D // HEAD_DIM // GROUP
    QW = GROUP * HEAD_DIM
    per_b = T // ATT_ROWS
    kcol = off["k"] // HEAD_DIM
    vcol = off["v"] // HEAD_DIM
    const = lambda b, g, i: (0, 0)
    return pl.pallas_call(
        _attn_kernel,
        grid=(B, KVH, per_b),
        in_specs=[pl.BlockSpec((ATT_ROWS, QW), lambda b, g, i: (b * per_b + i, g)),
                  pl.BlockSpec((L, HEAD_DIM), lambda b, g, i: ((B * T) // L + b, kcol + g)),
                  pl.BlockSpec((T, HEAD_DIM), lambda b, g, i: (b, kcol + g)),
                  pl.BlockSpec((L, HEAD_DIM), lambda b, g, i: ((B * T) // L + b, vcol + g)),
                  pl.BlockSpec((T, HEAD_DIM), lambda b, g, i: (b, vcol + g)),
                  pl.BlockSpec((ATT_ROWS, HEAD_DIM), lambda b, g, i: (i, 0)),
                  pl.BlockSpec((ATT_ROWS, HEAD_DIM), lambda b, g, i: (i, 0)),
                  pl.BlockSpec((L + T, HEAD_DIM), const),
                  pl.BlockSpec((L + T, HEAD_DIM), const),
                  pl.BlockSpec((1, HEAD_DIM), const),
                  pl.BlockSpec((1, HEAD_DIM), const)],
        out_specs=pl.BlockSpec((ATT_ROWS, QW), lambda b, g, i: (b * per_b + i, g)),
        out_shape=jax.ShapeDtypeStruct((B * T, D), BF16),
        scratch_shapes=[pltpu.VMEM((L + T, HEAD_DIM), BF16), pltpu.VMEM((L + T, 2 * HEAD_DIM), BF16)],
        compiler_params=_cparams(("arbitrary", "arbitrary", "arbitrary")),
        name="attention",
    )(P, P, P, P, P, cos_t[L:], sin_t[L:], cos_t, sin_t, qg, kg)


def _attention_ctx(P, qg, kg, B, T, L, D, off):
    KVH = D // HEAD_DIM // GROUP
    QW = GROUP * HEAD_DIM
    ctx0 = (B * T) // L
    kcol = off["k"] // HEAD_DIM
    vcol = off["v"] // HEAD_DIM
    const = lambda b, g: (0, 0)
    return pl.pallas_call(
        _attn_ctx_kernel,
        grid=(B, KVH),
        in_specs=[pl.BlockSpec((L, QW), lambda b, g: (ctx0 + b, g)),
                  pl.BlockSpec((L, HEAD_DIM), lambda b, g: (ctx0 + b, kcol + g)),
                  pl.BlockSpec((L, HEAD_DIM), lambda b, g: (ctx0 + b, vcol + g)),
                  pl.BlockSpec((1, HEAD_DIM), const),
                  pl.BlockSpec((1, HEAD_DIM), const)],
        out_specs=pl.BlockSpec((L, QW), lambda b, g: (b, g)),
        out_shape=jax.ShapeDtypeStruct((B * L, D), BF16),
        compiler_params=_cparams(("arbitrary", "arbitrary")),
        name="attention_ctx",
    )(P, P, P, qg, kg)


def _rope_tables(T, L):
    q = HEAD_DIM // 4
    rows = T // GRID_W
    row = jnp.repeat(jnp.arange(rows, dtype=F32), GRID_W)
    col = jnp.tile(jnp.arange(GRID_W, dtype=F32), rows)
    inv_freq = ROPE_THETA ** (-jnp.arange(0, 2 * q, 2, dtype=F32) / (2 * q))
    ar, ac = row[:, None] * inv_freq, col[:, None] * inv_freq
    cos = jnp.concatenate([jnp.cos(ar), jnp.cos(ar), jnp.cos(ac), jnp.cos(ac)], axis=1)
    sin = jnp.concatenate([-jnp.sin(ar), jnp.sin(ar), -jnp.sin(ac), jnp.sin(ac)], axis=1)
    cos = jnp.concatenate([jnp.ones((L, HEAD_DIM), F32), cos], axis=0)
    sin = jnp.concatenate([jnp.zeros((L, HEAD_DIM), F32), sin], axis=0)
    return cos, sin


def _even_layer(xa, mod3, last, B, T, L, alpha, w_in, w_out, pool_w, pool_scale, mu_rkv, mu_lora, w0, w2,
                a0, a2, k_k, k_a, r_k, gn_g, gn_b, ln_g, ln_b):
    R, D = xa.shape
    PW = pool_scale.shape[-1]
    RW = k_k.shape[-1]
    n_split = PW + 3 * RW
    lora2 = 2 * LORA_W
    w = jnp.concatenate([w_in[:, :n_split], w_in[:, n_split + 2 * lora2:],
                         w_in[:, n_split:n_split + 2 * lora2]], axis=1).astype(BF16)
    off = {"a": 0, "r": PW, "k": PW + RW, "v": PW + 2 * RW, "gate": n_split,
           "wd": n_split + PW + RW, "ad": n_split + PW + RW + lora2}
    P = _projection(xa, mod3, w, B, T, _proj_tn(w.shape[1]))

    ys = []
    for d, rev in enumerate((False, True)):
        vecs = jnp.zeros((16, RW), F32)
        rows = [mu_rkv[d, 0], mu_rkv[d, 1], mu_rkv[d, 2], w0[d], a0[d], k_k, k_a, r_k.reshape(RW),
                gn_g[d], gn_b[d]]
        vecs = vecs.at[:len(rows)].set(jnp.stack(rows))
        lvec = jnp.zeros((8, lora2), F32).at[0].set(mu_lora[:, 0, :].reshape(lora2)).at[1].set(
            mu_lora[:, 1, :].reshape(lora2))
        w2p = jnp.zeros((lora2, RW), F32).at[d * LORA_W:(d + 1) * LORA_W].set(w2[d]).astype(BF16)
        a2p = jnp.zeros((lora2, RW), F32).at[d * LORA_W:(d + 1) * LORA_W].set(a2[d]).astype(BF16)
        ys.append(_wkv_direction(P, rev, B, T, L, RW, off, vecs, lvec, w2p, a2p))

    n_rows = B * T if last else R
    return _even_out(xa, P, ys[0], ys[1], mod3, pool_w.astype(BF16), pool_scale.reshape(1, PW),
                     w_out.astype(BF16), ln_g.reshape(1, D), ln_b.reshape(1, D), alpha, B, T, L, n_rows, off)


def _odd_layer(xa, mod3, last, B, T, L, alpha, w_in, w_out, q_g, k_g, tables, ln_g, ln_b):
    R, D = xa.shape
    KVW = D // GROUP
    w = jnp.concatenate([w_in[:, :D], w_in[:, D + 2 * KVW:], w_in[:, D:D + 2 * KVW]], axis=1).astype(BF16)
    off = {"q": 0, "gate": D, "k": 2 * D, "v": 2 * D + KVW}
    P = _projection(xa, mod3, w, B, T, _proj_tn(w.shape[1]))
    qg, kg = q_g.reshape(1, HEAD_DIM), k_g.reshape(1, HEAD_DIM)
    att = _attention(P, tables[0], tables[1], qg, kg, B, T, L, D, off)
    att_ctx = None if last else _attention_ctx(P, qg, kg, B, T, L, D, off)
    n_rows = B * T if last else R
    return _odd_out(xa, P, att, att_ctx, mod3, w_out.astype(BF16), ln_g.reshape(1, D), ln_b.reshape(1, D),
                    alpha, B, T, n_rows, off)


def kernel(x, c, ctx, c_ctx, mod_w, mod_b, ln_g, ln_b, ev_w_in, ev_w_out, pool_w, pool_scale, rw_mu_rkv,
           rw_mu_lora, rw_w0, rw_w2, rw_a0, rw_a2, rw_k_k, rw_k_a, rw_r_k, rw_gn_g, rw_gn_b, od_w_in,
           od_w_out, q_norm_g, k_norm_g):
    B, T, D = x.shape
    L = ctx.shape[1]
    depth = mod_w.shape[0]
    assert L == BLK and T % TM == 0 and (B * L) % TM == 0 and T % GRID_W == 0 and B < MOD_ROWS
    assert D // HEAD_DIM % GROUP == 0 and (D // 2) % 256 == 0 and L % WKV_ROWS == 0 and 2 * CH == LANES
    alpha = (2 * depth) ** 0.25
    xa = jnp.concatenate([x.reshape(B * T, D), ctx.reshape(B * L, D)], axis=0)
    mod = _modulation(c, c_ctx, mod_w, mod_b)
    tables = _rope_tables(T, L)
    for l in range(depth):
        last = l == depth - 1
        i = l // 2
        mod3 = mod[l].reshape(MOD_ROWS, 1, 3 * D)
        if l % 2 == 0:
            xa = _even_layer(xa, mod3, last, B, T, L, alpha, ev_w_in[i], ev_w_out[i], pool_w[i], pool_scale[i],
                             rw_mu_rkv[i], rw_mu_lora[i], rw_w0[i], rw_w2[i], rw_a0[i], rw_a2[i], rw_k_k[i],
                             rw_k_a[i], rw_r_k[i], rw_gn_g[i], rw_gn_b[i], ln_g[l], ln_b[l])
        else:
            xa = _odd_layer(xa, mod3, last, B, T, L, alpha, od_w_in[i], od_w_out[i], q_norm_g[i], k_norm_g[i],
                            tables, ln_g[l], ln_b[l])
    return xa[:B * T].reshape(B, T, D)
```
